```python
import math
import jax, jax.numpy as jnp
from jax import lax
import numpy as np

D_MODEL = 1024
BATCH = 8
SEQ = 4096
DEPTH = 4

GRID_W = 64
CTX_LEN = 256
N_MOD = 9
D_FF = 2816
CONV_DIM = 512
CONV_WIDTH = 31
GDN_HEADS = 4
GDN_DK = 128
GDN_DV = 128
GDN_DIM = GDN_HEADS * GDN_DV
SHORT_CONV = 5
GDN_CHUNK = 64
EVEN_IN = 2 * CONV_DIM + 4 * GDN_DIM + 4 * GDN_HEADS
EVEN_MIX = CONV_DIM + GDN_DIM
S5_DIM = 512
S5_GROUP_CH = 16
S5_GROUPS = S5_DIM // S5_GROUP_CH
S5_STATE = 64
EPS = 1e-6

kernel_name = 'hybrid_conv_deltanet_s5_prefix_trunk'


def rms_norm(x, g):
    xf = x.astype(jnp.float32)
    y = xf * lax.rsqrt(jnp.mean(xf * xf, axis=-1, keepdims=True) + EPS)
    return (y * g.astype(jnp.float32)).astype(x.dtype)


def layer_norm(x, g, b):
    xf = x.astype(jnp.float32)
    xc = xf - jnp.mean(xf, axis=-1, keepdims=True)
    y = xc * lax.rsqrt(jnp.mean(xc * xc, axis=-1, keepdims=True) + EPS)
    return (y * g.astype(jnp.float32) + b.astype(jnp.float32)).astype(x.dtype)


def modulate(x, g, shift, scale):
    return rms_norm(x, g) * (1 + scale) + shift


def swiglu(x, w13, w2):
    a, b = jnp.split(x @ w13, 2, axis=-1)
    return (jax.nn.silu(a) * b) @ w2


def depthwise_conv(x, w):
    k = w.shape[0]
    return lax.conv_general_dilated(x, w[:, None, :], window_strides=(1,),
                                    padding=[(k // 2, k // 2)],
                                    dimension_numbers=('NWC', 'WIO', 'NWC'),
                                    feature_group_count=x.shape[-1])


def conformer_conv(glu_in, conv_w, conv_b, ln_g, ln_b, rows):
    a, g = jnp.split(glu_in, 2, axis=-1)
    h = a * jax.nn.sigmoid(g)
    if rows is None:
        h = depthwise_conv(h, conv_w)
    else:
        bsz, seqlen, ch = h.shape
        h = depthwise_conv(h.reshape(bsz * rows, GRID_W, ch), conv_w).reshape(bsz, seqlen, ch)
    return jax.nn.silu(layer_norm(h + conv_b, ln_g, ln_b))


def l2norm(t):
    return t * lax.rsqrt(jnp.sum(t * t, axis=-1, keepdims=True) + EPS)


def gdn_prepare(qkv_raw, ab_raw, sconv_w, a_log, dt_bias):
    bsz, seqlen, _ = qkv_raw.shape
    qkv = jax.nn.silu(depthwise_conv(qkv_raw, sconv_w)).astype(jnp.float32)
    q, k, v = jnp.split(qkv, 3, axis=-1)
    q = l2norm(q.reshape(bsz, seqlen, GDN_HEADS, GDN_DK)) * (GDN_DK ** -0.5)
    k = l2norm(k.reshape(bsz, seqlen, GDN_HEADS, GDN_DK))
    v = v.reshape(bsz, seqlen, GDN_HEADS, GDN_DV)
    ab = ab_raw.astype(jnp.float32).reshape(bsz, seqlen, 2, 2, GDN_HEADS)
    log_a = -jnp.exp(a_log.astype(jnp.float32)) * jax.nn.softplus(ab[:, :, :, 0] + dt_bias.astype(jnp.float32))
    beta = jax.nn.sigmoid(ab[:, :, :, 1])
    return q, k, v, log_a, beta


def gated_delta_rule(q, k, v, log_a, beta, s0, with_out):
    bsz, seqlen, nh, _ = q.shape
    n = seqlen // GDN_CHUNK

    def to_chunks(t):
        t = t.reshape((bsz, n, GDN_CHUNK, nh) + t.shape[3:])
        return jnp.moveaxis(jnp.moveaxis(t, 1, 0), 3, 2)

    q, k, v, log_a, beta = (to_chunks(t) for t in (q, k, v, log_a, beta))
    g = jnp.cumsum(log_a, axis=-1)
    idx = jnp.arange(GDN_CHUNK)
    lower = idx[:, None] >= idx[None, :]
    strict = idx[:, None] > idx[None, :]
    gdiff = g[..., :, None] - g[..., None, :]
    decay = jnp.where(lower, jnp.exp(jnp.where(lower, gdiff, 0.0)), 0.0)
    kb = k * beta[..., None]
    a_mat = jnp.where(strict, jnp.einsum('nbhid,nbhjd->nbhij', kb, k) * decay, 0.0)
    eye = jnp.eye(GDN_CHUNK, dtype=a_mat.dtype)
    t_mat = lax.linalg.triangular_solve(a_mat + eye, jnp.broadcast_to(eye, a_mat.shape),
                                        left_side=True, lower=True, unit_diagonal=True)
    u = jnp.einsum('nbhij,nbhjd->nbhid', t_mat, v * beta[..., None])
    w = jnp.einsum('nbhij,nbhjd->nbhid', t_mat, kb * jnp.exp(g)[..., None])
    g_last = g[..., -1]
    k_tail = k * jnp.exp(g_last[..., None] - g)[..., None]

    def step(s, inp):
        w_i, u_i, kt_i, gl_i = inp[:4]
        v_new = u_i - jnp.einsum('bhck,bhkv->bhcv', w_i, s)
        s_next = s * jnp.exp(gl_i)[..., None, None] + jnp.einsum('bhck,bhcv->bhkv', kt_i, v_new)
        if not with_out:
            return s_next, None
        qd_i, qk_i = inp[4:]
        o = jnp.einsum('bhck,bhkv->bhcv', qd_i, s) + jnp.einsum('bhij,bhjv->bhiv', qk_i, v_new)
        return s_next, o

    xs = (w, u, k_tail, g_last)
    if with_out:
        xs = xs + (q * jnp.exp(g)[..., None], jnp.einsum('nbhid,nbhjd->nbhij', q, k) * decay)
    s_final, o = lax.scan(step, s0, xs)
    if not with_out:
        return s_final, None
    o = jnp.swapaxes(jnp.moveaxis(o, 0, 1), 2, 3).reshape(bsz, seqlen, nh, GDN_DV)
    return s_final, o


def bidir_gdn(prep_c, prep_l, with_ctx_out):
    qc, kc, vc, lac, bec = prep_c
    ql, kl, vl, lal, bel = prep_l
    s_zero = jnp.zeros((qc.shape[0], GDN_HEADS, GDN_DK, GDN_DV), jnp.float32)
    fl = lambda t: jnp.flip(t, axis=1)
    s_cf, o_cf = gated_delta_rule(qc, kc, vc, lac[:, :, 0], bec[:, :, 0], s_zero, with_ctx_out)
    _, o_lf = gated_delta_rule(ql, kl, vl, lal[:, :, 0], bel[:, :, 0], s_cf, True)
    s_cb, o_cb = gated_delta_rule(fl(qc), fl(kc), fl(vc), fl(lac[:, :, 1]), fl(bec[:, :, 1]), s_zero, with_ctx_out)
    _, o_lb = gated_delta_rule(fl(ql), fl(kl), fl(vl), fl(lal[:, :, 1]), fl(bel[:, :, 1]), s_cb, True)
    o_l = o_lf + fl(o_lb)
    o_c = o_cf + fl(o_cb) if with_ctx_out else None
    return o_c, o_l


def gdn_output(o, gate, g):
    o = o * lax.rsqrt(jnp.mean(o * o, axis=-1, keepdims=True) + EPS) * g.astype(jnp.float32)
    o = o * jax.nn.silu(gate.astype(jnp.float32).reshape(o.shape))
    return o.reshape(o.shape[0], o.shape[1], GDN_DIM)


def even_mixer(hc, hl, w_in, conv_w, conv_b, cln_g, cln_b, sconv_w, a_log, dt_bias, onorm_g, w_out,
               rows, with_ctx_out):
    cuts = [2 * CONV_DIM, 2 * CONV_DIM + 3 * GDN_DIM, 2 * CONV_DIM + 4 * GDN_DIM]
    glu_c, qkv_c, gate_c, ab_c = jnp.split(hc @ w_in, cuts, axis=-1)
    glu_l, qkv_l, gate_l, ab_l = jnp.split(hl @ w_in, cuts, axis=-1)
    prep_c = gdn_prepare(qkv_c, ab_c, sconv_w, a_log, dt_bias)
    prep_l = gdn_prepare(qkv_l, ab_l, sconv_w, a_log, dt_bias)
    o_c, o_l = bidir_gdn(prep_c, prep_l, with_ctx_out)

    def merge(glu, o, gate, r):
        conv = conformer_conv(glu, conv_w, conv_b, cln_g, cln_b, r)
        gdn = gdn_output(o, gate, onorm_g).astype(conv.dtype)
        return jnp.concatenate([conv, gdn], axis=-1) @ w_out

    yl = merge(glu_l, o_l, gate_l, rows)
    yc = merge(glu_c, o_c, gate_c, None) if with_ctx_out else None
    return yc, yl


def s5_discretize(lam_re, lam_im, log_step, b_re, b_im):
    lam_re = jnp.minimum(lam_re, -1e-4)
    dt = jnp.exp(log_step)[:, None]
    mag = jnp.exp(lam_re * dt)
    ab_re = mag * jnp.cos(lam_im * dt)
    ab_im = mag * jnp.sin(lam_im * dt)
    den = lam_re * lam_re + lam_im * lam_im
    num_re = ab_re - 1.0
    coef_re = (num_re * lam_re + ab_im * lam_im) / den
    coef_im = (ab_im * lam_re - num_re * lam_im) / den
    bb_re = coef_re[..., None] * b_re - coef_im[..., None] * b_im
    bb_im = coef_re[..., None] * b_im + coef_im[..., None] * b_re
    return ab_re, ab_im, bb_re, bb_im


def s5_input(u, bb_re, bb_im):
    return (jnp.einsum('lbgh,gph->lbgp', u, bb_re), jnp.einsum('lbgh,gph->lbgp', u, bb_im))


def _ssm_combine(e1, e2):
    a1r, a1i, b1r, b1i = e1
    a2r, a2i, b2r, b2i = e2
    return (a2r * a1r - a2i * a1i, a2r * a1i + a2i * a1r,
            a2r * b1r - a2i * b1i + b2r, a2r * b1i + a2i * b1r + b2i)


def s5_scan(ab_re, ab_im, bu_re, bu_im, reverse):
    shape = (bu_re.shape[0], 1) + ab_re.shape
    a_re = jnp.broadcast_to(ab_re, shape)
    a_im = jnp.broadcast_to(ab_im, shape)
    _, _, x_re, x_im = lax.associative_scan(_ssm_combine, (a_re, a_im, bu_re, bu_im), reverse=reverse, axis=0)
    return x_re, x_im


def s5_readout(x_re, x_im, c_re, c_im):
    return jnp.einsum('lbgp,ghp->lbgh', x_re, c_re) - jnp.einsum('lbgp,ghp->lbgh', x_im, c_im)


def s5_mixer(hc, hl, w_in, lam_re, lam_im, log_step, b_re, b_im, c_re, c_im, d_skip, w_out, with_ctx_out):
    f32 = jnp.float32

    def to_groups(h):
        u = (h @ w_in).astype(f32)
        bsz, seqlen, _ = u.shape
        return jnp.swapaxes(u, 0, 1).reshape(seqlen, bsz, S5_GROUPS, S5_GROUP_CH)

    uc, ul = to_groups(hc), to_groups(hl)
    d = d_skip.astype(f32).reshape(S5_GROUPS, S5_GROUP_CH)
    yl = d * ul
    yc = d * uc if with_ctx_out else None
    for direction in range(2):
        rev = direction == 1
        ab_re, ab_im, bb_re, bb_im = s5_discretize(lam_re[direction].astype(f32), lam_im[direction].astype(f32),
                                                   log_step[direction].astype(f32), b_re[direction].astype(f32),
                                                   b_im[direction].astype(f32))
        bc_re, bc_im = s5_input(uc, bb_re, bb_im)
        xc_re, xc_im = s5_scan(ab_re, ab_im, bc_re, bc_im, rev)
        end = 0 if rev else -1
        start = -1 if rev else 0
        h0_re, h0_im = xc_re[end], xc_im[end]
        bl_re, bl_im = s5_input(ul, bb_re, bb_im)
        bl_re = bl_re.at[start].add(ab_re * h0_re - ab_im * h0_im)
        bl_im = bl_im.at[start].add(ab_re * h0_im + ab_im * h0_re)
        xl_re, xl_im = s5_scan(ab_re, ab_im, bl_re, bl_im, rev)
        cr, ci = c_re[direction].astype(f32), c_im[direction].astype(f32)
        yl = yl + s5_readout(xl_re, xl_im, cr, ci)
        if with_ctx_out:
            yc = yc + s5_readout(xc_re, xc_im, cr, ci)

    def glu_out(y):
        seqlen, bsz = y.shape[:2]
        z = jax.nn.gelu(jnp.swapaxes(y.reshape(seqlen, bsz, S5_DIM), 0, 1)).astype(hl.dtype)
        a, g = jnp.split(z @ w_out, 2, axis=-1)
        return a * jax.nn.sigmoid(g)

    return (glu_out(yc) if with_ctx_out else None), glu_out(yl)


def setup_inputs(seed: int = 0) -> dict:
    key = jax.random.key(seed)
    ks = iter(jax.random.split(key, 32))
    f32 = jnp.float32
    nrm = lambda shape, scale: jax.random.normal(next(ks), shape, f32) * scale
    ne = (DEPTH + 1) // 2
    no = DEPTH // 2
    D = D_MODEL
    inp = {}
    inp['x'] = nrm((BATCH, SEQ, D), 1.0)
    inp['c'] = nrm((BATCH, D), 1.0)
    inp['ctx'] = nrm((BATCH, CTX_LEN, D), 1.0)
    inp['c_ctx'] = nrm((D,), 1.0)
    inp['w_mod'] = nrm((DEPTH, D, N_MOD * D), 0.5 * D ** -0.5)
    inp['b_mod'] = nrm((DEPTH, N_MOD * D), 0.01)
    inp['norm_g'] = 1.0 + nrm((DEPTH, 3, D), 0.02)
    inp['ffn_w13'] = nrm((DEPTH, 2, D, 2 * D_FF), D ** -0.5)
    inp['ffn_w2'] = nrm((DEPTH, 2, D_FF, D), D_FF ** -0.5)
    inp['final_g'] = 1.0 + nrm((D,), 0.02)
    inp['e_w_in'] = nrm((ne, D, EVEN_IN), D ** -0.5)
    inp['e_conv_w'] = nrm((ne, CONV_WIDTH, CONV_DIM), CONV_WIDTH ** -0.5)
    inp['e_conv_b'] = nrm((ne, CONV_DIM), 0.01)
    inp['e_cln_g'] = 1.0 + nrm((ne, CONV_DIM), 0.02)
    inp['e_cln_b'] = nrm((ne, CONV_DIM), 0.01)
    inp['e_sconv_w'] = nrm((ne, SHORT_CONV, 3 * GDN_DIM), SHORT_CONV ** -0.5)
    inp['e_a_log'] = jnp.log(jax.random.uniform(next(ks), (ne, 2, GDN_HEADS), f32, 1.0, 16.0))
    dt = jnp.exp(jax.random.uniform(next(ks), (ne, 2, GDN_HEADS), f32, math.log(1e-3), math.log(1e-1)))
    inp['e_dt_bias'] = dt + jnp.log(-jnp.expm1(-dt))
    inp['e_onorm_g'] = 1.0 + nrm((ne, GDN_DV), 0.02)
    inp['e_w_out'] = nrm((ne, EVEN_MIX, D), EVEN_MIX ** -0.5)
    inp['o_w_in'] = nrm((no, D, S5_DIM), D ** -0.5)
    inp['o_lam_re'] = -0.5 + nrm((no, 2, S5_GROUPS, S5_STATE), 0.01)
    inp['o_lam_im'] = jnp.pi * jnp.arange(S5_STATE, dtype=f32) + nrm((no, 2, S5_GROUPS, S5_STATE), 0.01)
    inp['o_log_step'] = jax.random.uniform(next(ks), (no, 2, S5_GROUPS), f32, math.log(1e-3), math.log(1e-1))
    inp['o_b_re'] = nrm((no, 2, S5_GROUPS, S5_STATE, S5_GROUP_CH), (2 * S5_GROUP_CH) ** -0.5)
    inp['o_b_im'] = nrm((no, 2, S5_GROUPS, S5_STATE, S5_GROUP_CH), (2 * S5_GROUP_CH) ** -0.5)
    inp['o_c_re'] = nrm((no, 2, S5_GROUPS, S5_GROUP_CH, S5_STATE), (2 * S5_STATE) ** -0.5)
    inp['o_c_im'] = nrm((no, 2, S5_GROUPS, S5_GROUP_CH, S5_STATE), (2 * S5_STATE) ** -0.5)
    inp['o_d'] = nrm((no, S5_DIM), 1.0)
    inp['o_w_out'] = nrm((no, S5_DIM, 2 * D), S5_DIM ** -0.5)
    return inp


def reference(x, c, ctx, c_ctx, w_mod, b_mod, norm_g, ffn_w13, ffn_w2, final_g,
              e_w_in, e_conv_w, e_conv_b, e_cln_g, e_cln_b, e_sconv_w, e_a_log, e_dt_bias, e_onorm_g, e_w_out,
              o_w_in, o_lam_re, o_lam_im, o_log_step, o_b_re, o_b_im, o_c_re, o_c_im, o_d, o_w_out):
    rows = x.shape[1] // GRID_W
    for l in range(DEPTH):
        last = l == DEPTH - 1
        j = l // 2
        ml = jnp.split((jax.nn.silu(c) @ w_mod[l] + b_mod[l])[:, None, :], N_MOD, axis=-1)
        mc = jnp.split(jax.nn.silu(c_ctx) @ w_mod[l] + b_mod[l], N_MOD, axis=-1)
        x = x + 0.5 * ml[2] * swiglu(modulate(x, norm_g[l, 0], ml[0], ml[1]), ffn_w13[l, 0], ffn_w2[l, 0])
        ctx = ctx + 0.5 * mc[2] * swiglu(modulate(ctx, norm_g[l, 0], mc[0], mc[1]), ffn_w13[l, 0], ffn_w2[l, 0])
        hl = modulate(x, norm_g[l, 1], ml[3], ml[4])
        hc = modulate(ctx, norm_g[l, 1], mc[3], mc[4])
        if l % 2 == 0:
            yc, yl = even_mixer(hc, hl, e_w_in[j], e_conv_w[j], e_conv_b[j], e_cln_g[j], e_cln_b[j],
                                e_sconv_w[j], e_a_log[j], e_dt_bias[j], e_onorm_g[j], e_w_out[j],
                                rows, not last)
        else:
            yc, yl = s5_mixer(hc, hl, o_w_in[j], o_lam_re[j], o_lam_im[j], o_log_step[j], o_b_re[j], o_b_im[j],
                              o_c_re[j], o_c_im[j], o_d[j], o_w_out[j], not last)
        x = x + ml[5] * yl
        x = x + 0.5 * ml[8] * swiglu(modulate(x, norm_g[l, 2], ml[6], ml[7]), ffn_w13[l, 1], ffn_w2[l, 1])
        if not last:
            ctx = ctx + mc[5] * yc
            ctx = ctx + 0.5 * mc[8] * swiglu(modulate(ctx, norm_g[l, 2], mc[6], mc[7]), ffn_w13[l, 1], ffn_w2[l, 1])
    return rms_norm(x, final_g)
```

```python
import functools
import math

import jax
import jax.numpy as jnp
from jax import lax
from jax.experimental import pallas as pl
from jax.experimental.pallas import tpu as pltpu

EPS = 1e-6
TILE = 256
LANES = 128
SUBLANES = 8
GRID_W = 64
GDN_HEADS = 4
GDN_DK = 128
S5_GROUP_CH = 16
S5_STATE = 64
S5_BLOCK_GROUPS = 8
S5_STEPS = 64
MOD_ROWS = 16
VMEM_LIMIT = 56 * 1024 * 1024

F32 = jnp.float32
BF16 = jnp.bfloat16


def _cparams(*sem):
    return pltpu.CompilerParams(dimension_semantics=sem, vmem_limit_bytes=VMEM_LIMIT)


def _dot(a, b):
    return jnp.dot(a.astype(BF16), b.astype(BF16), preferred_element_type=F32)


def _dot_nt(a, b):
    return lax.dot_general(a.astype(BF16), b.astype(BF16), (((1,), (1,)), ((), ())),
                           preferred_element_type=F32)


def _dot_exact(a, b):
    return jnp.dot(a, b, precision=lax.Precision.HIGHEST, preferred_element_type=F32)


def _sigmoid(x):
    return 1.0 / (1.0 + jnp.exp(-x))


def _silu(x):
    return x * _sigmoid(x)


def _mod_row(mod_ref, row, k, d):
    return mod_ref[pl.ds(row, 1), pl.ds(k * d, d)]


def _modulated_norm(x, g, shift, scale):
    y = x * lax.rsqrt(jnp.mean(x * x, axis=-1, keepdims=True) + EPS)
    return (y * g) * (1.0 + scale) + shift


def _full(shape):
    n = len(shape)
    return pl.BlockSpec(shape, lambda *_: (0,) * n)


def _mod_kernel(c_ref, w_ref, b_ref, o_ref):
    h = _silu(c_ref[...])
    o_ref[0] = _dot(h, w_ref[0]) + b_ref[0]


def _mod_table(cc, w_mod, b_mod):
    depth, d, nd = w_mod.shape
    tn = 1024
    return pl.pallas_call(
        _mod_kernel,
        out_shape=jax.ShapeDtypeStruct((depth, MOD_ROWS, nd), F32),
        grid=(depth, nd // tn),
        in_specs=[pl.BlockSpec((MOD_ROWS, d), lambda l, j: (0, 0)),
                  pl.BlockSpec((1, d, tn), lambda l, j: (l, 0, j)),
                  pl.BlockSpec((1, 1, tn), lambda l, j: (l, 0, j))],
        out_specs=pl.BlockSpec((1, MOD_ROWS, tn), lambda l, j: (l, 0, j)),
        compiler_params=_cparams("parallel", "parallel"),
        name="mod_table",
    )(cc, w_mod, b_mod.reshape(depth, 1, nd))


def _ffn_kernel(x_ref, mod_ref, g_ref, w1_ref, w3_ref, w2_ref, fg_ref, o_ref, t_ref, *,
                nct, ctx_row, k0, fchunk, final):
    d = x_ref.shape[-1]
    f = w1_ref.shape[-1]
    row = jnp.where(pl.program_id(1) < nct, ctx_row, pl.program_id(0))
    x = x_ref[0]
    h = _modulated_norm(x, g_ref[...], _mod_row(mod_ref, row, k0, d), _mod_row(mod_ref, row, k0 + 1, d))
    hb = h.astype(BF16)
    for c in range(f // fchunk):
        sl = slice(c * fchunk, (c + 1) * fchunk)
        a = jnp.dot(hb, w1_ref[:, sl], preferred_element_type=F32)
        b = jnp.dot(hb, w3_ref[:, sl], preferred_element_type=F32)
        t_ref[:, sl] = (_silu(a) * b).astype(BF16)
    y = jnp.dot(t_ref[...], w2_ref[...], preferred_element_type=F32)
    out = x + (0.5 * _mod_row(mod_ref, row, k0 + 2, d)) * y
    if final:
        out = out * lax.rsqrt(jnp.mean(out * out, axis=-1, keepdims=True) + EPS) * fg_ref[...]
    o_ref[0] = out


def _ffn(xs, mod, g, w1, w3, w2, final_g, *, nct, k0, final):
    bsz, lt, d = xs.shape
    f = w1.shape[-1]
    nt = lt // TILE
    kern = functools.partial(_ffn_kernel, nct=nct, ctx_row=bsz, k0=k0, fchunk=256, final=final)
    if final:
        out_shape = jax.ShapeDtypeStruct((bsz, lt - nct * TILE, d), F32)
        out_spec = pl.BlockSpec((1, TILE, d), lambda b, t: (b, jnp.maximum(t - nct, 0), 0))
        aliases = {}
    else:
        out_shape = jax.ShapeDtypeStruct(xs.shape, F32)
        out_spec = pl.BlockSpec((1, TILE, d), lambda b, t: (b, t, 0))
        aliases = {0: 0}
    return pl.pallas_call(
        kern,
        out_shape=out_shape,
        grid=(bsz, nt),
        in_specs=[pl.BlockSpec((1, TILE, d), lambda b, t: (b, t, 0)),
                  _full(mod.shape), _full((1, d)),
                  _full((d, f)), _full((d, f)), _full((f, d)), _full((1, d))],
        out_specs=out_spec,
        scratch_shapes=[pltpu.VMEM((TILE, f), BF16)],
        input_output_aliases=aliases,
        compiler_params=_cparams("parallel", "arbitrary"),
        name="ffn_final" if final else "ffn",
    )(xs, mod, g.reshape(1, d), w1, w3, w2, final_g.reshape(1, d))


def _depthwise_rows(h, pad_ref, w_ref, seg):
    n, _ = h.shape
    kw = w_ref.shape[0]
    half = kw // 2
    lead = -(-half // SUBLANES) * SUBLANES
    stride = seg + 2 * lead
    outs = []
    for s in range(n // seg):
        base = s * stride
        zeros = jnp.zeros((lead, h.shape[1]), F32)
        pad_ref[base:base + lead, :] = zeros
        pad_ref[base + lead:base + lead + seg, :] = h[s * seg:(s + 1) * seg]
        pad_ref[base + lead + seg:base + stride, :] = zeros
    for s in range(n // seg):
        base = s * stride + lead - half
        acc = w_ref[0:1, :] * pad_ref[base:base + seg, :]
        for k in range(1, kw):
            acc = acc + w_ref[k:k + 1, :] * pad_ref[base + k:base + k + seg, :]
        outs.append(acc)
    return outs[0] if len(outs) == 1 else jnp.concatenate(outs, axis=0)


def _even_in_kernel(x_ref, mod_ref, g_ref, wglu_ref, wqkv_ref, wgate_ref, wab_ref,
                    cw_ref, cb_ref, lg_ref, lb_ref,
                    conv_ref, qkv_ref, gate_ref, ab_ref, pad_ref, *, nct, ctx_row):
    d = x_ref.shape[-1]
    cdim = cw_ref.shape[-1]
    is_ctx = pl.program_id(1) < nct
    row = jnp.where(is_ctx, ctx_row, pl.program_id(0))
    h = _modulated_norm(x_ref[0], g_ref[...], _mod_row(mod_ref, row, 3, d), _mod_row(mod_ref, row, 4, d))
    hb = h.astype(BF16)
    qkv_ref[0] = jnp.dot(hb, wqkv_ref[...], preferred_element_type=F32)
    gate_ref[0] = jnp.dot(hb, wgate_ref[...], preferred_element_type=F32)
    ab_ref[0] = jnp.dot(hb, wab_ref[...], preferred_element_type=F32)[:, :ab_ref.shape[-1]]
    glu = jnp.dot(hb, wglu_ref[...], preferred_element_type=F32)
    hc = glu[:, :cdim] * _sigmoid(glu[:, cdim:])

    def finish(conv):
        y = conv + cb_ref[...]
        yc = y - jnp.mean(y, axis=-1, keepdims=True)
        yn = yc * lax.rsqrt(jnp.mean(yc * yc, axis=-1, keepdims=True) + EPS)
        conv_ref[0] = _silu(yn * lg_ref[...] + lb_ref[...])

    @pl.when(is_ctx)
    def _():
        finish(_depthwise_rows(hc, pad_ref, cw_ref, TILE))

    @pl.when(jnp.logical_not(is_ctx))
    def _():
        finish(_depthwise_rows(hc, pad_ref, cw_ref, GRID_W))


def _even_in(xs, mod, g, wglu, wqkv, wgate, wab, conv_w, conv_b, cln_g, cln_b, *, nct, n_ab):
    bsz, lt, d = xs.shape
    nt = lt // TILE
    cdim = conv_w.shape[-1]
    kw = conv_w.shape[0]
    lead = -(-(kw // 2) // SUBLANES) * SUBLANES
    pad_rows = max(TILE + 2 * lead, (TILE // GRID_W) * (GRID_W + 2 * lead))
    tok = lambda n: pl.BlockSpec((1, TILE, n), lambda b, t: (b, t, 0))
    kern = functools.partial(_even_in_kernel, nct=nct, ctx_row=bsz)
    return pl.pallas_call(
        kern,
        out_shape=[jax.ShapeDtypeStruct((bsz, lt, cdim), F32),
                   jax.ShapeDtypeStruct((bsz, lt, wqkv.shape[-1]), F32),
                   jax.ShapeDtypeStruct((bsz, lt, wgate.shape[-1]), F32),
                   jax.ShapeDtypeStruct((bsz, lt, n_ab), F32)],
        grid=(bsz, nt),
        in_specs=[tok(d), _full(mod.shape), _full((1, d)),
                  _full(wglu.shape), _full(wqkv.shape), _full(wgate.shape), _full(wab.shape),
                  _full(conv_w.shape), _full((1, cdim)), _full((1, cdim)), _full((1, cdim))],
        out_specs=[tok(cdim), tok(wqkv.shape[-1]), tok(wgate.shape[-1]), tok(n_ab)],
        scratch_shapes=[pltpu.VMEM((pad_rows, cdim), F32)],
        compiler_params=_cparams("parallel", "arbitrary"),
        name="even_in",
    )(xs, mod, g.reshape(1, d), wglu, wqkv, wgate, wab,
      conv_w, conv_b.reshape(1, cdim), cln_g.reshape(1, cdim), cln_b.reshape(1, cdim))


def _gdn_prep_kernel(qkv_ref, prev_ref, next_ref, ab_ref, sw_ref, alog_ref, dtb_ref,
                     q_ref, k_ref, v_ref, gf_ref, gb_ref, gt_ref, beta_ref, pad_ref, *, nct, nt):
    t = pl.program_id(1)
    n = qkv_ref.shape[1]
    c = qkv_ref.shape[2]
    kw = sw_ref.shape[0]
    half = kw // 2
    has_prev = jnp.logical_and(t != 0, t != nct)
    has_next = jnp.logical_and(t != nct - 1, t != nt - 1)
    pad_ref[0:SUBLANES, :] = jnp.where(has_prev, prev_ref[0], 0.0)
    pad_ref[SUBLANES:SUBLANES + n, :] = qkv_ref[0]
    pad_ref[SUBLANES + n:2 * SUBLANES + n, :] = jnp.where(has_next, next_ref[0], 0.0)
    base = SUBLANES - half
    acc = sw_ref[0:1, :] * pad_ref[base:base + n, :]
    for k in range(1, kw):
        acc = acc + sw_ref[k:k + 1, :] * pad_ref[base + k:base + k + n, :]
    y = _silu(acc)
    dim = c // 3
    for i, ref in enumerate((q_ref, k_ref, v_ref)):
        part = y[:, i * dim:(i + 1) * dim]
        if i == 2:
            ref[0] = part
            continue
        heads = []
        for hd in range(dim // GDN_DK):
            p = part[:, hd * GDN_DK:(hd + 1) * GDN_DK]
            p = p * lax.rsqrt(jnp.sum(p * p, axis=-1, keepdims=True) + EPS)
            heads.append(p * (GDN_DK ** -0.5) if i == 0 else p)
        ref[0] = jnp.concatenate(heads, axis=-1)
    ab = ab_ref[0]
    z = ab + dtb_ref[...]
    softplus = jnp.maximum(z, 0.0) + jnp.log1p(jnp.exp(-jnp.abs(z)))
    la = -jnp.exp(alog_ref[...]) * softplus
    ii = lax.broadcasted_iota(jnp.int32, (n, n), 0)
    jj = lax.broadcasted_iota(jnp.int32, (n, n), 1)
    gf_ref[0] = _dot_exact(jnp.where(ii >= jj, 1.0, 0.0), la)
    gb_ref[0] = _dot_exact(jnp.where(ii <= jj, 1.0, 0.0), la)
    gt_ref[0] = _dot_exact(jnp.ones((n, n), F32), la)
    beta_ref[0] = _sigmoid(ab)


def _gdn_prep(qkv, ab, sconv_w, alog_row, dtb_row, *, nct):
    bsz, lt, c = qkv.shape
    nt = lt // TILE
    dim = c // 3
    n_ab = ab.shape[-1]
    per = TILE // SUBLANES
    nblk = lt // SUBLANES
    tok = lambda n: pl.BlockSpec((1, TILE, n), lambda b, t: (b, t, 0))
    kern = functools.partial(_gdn_prep_kernel, nct=nct, nt=nt)
    return pl.pallas_call(
        kern,
        out_shape=[jax.ShapeDtypeStruct((bsz, lt, dim), F32)] * 3
        + [jax.ShapeDtypeStruct((bsz, lt, n_ab), F32)] * 4,
        grid=(bsz, nt),
        in_specs=[tok(c),
                  pl.BlockSpec((1, SUBLANES, c), lambda b, t: (b, jnp.maximum(t * per - 1, 0), 0)),
                  pl.BlockSpec((1, SUBLANES, c), lambda b, t: (b, jnp.minimum((t + 1) * per, nblk - 1), 0)),
                  tok(n_ab), _full(sconv_w.shape), _full((1, n_ab)), _full((1, n_ab))],
        out_specs=[tok(dim)] * 3 + [tok(n_ab)] * 4,
        scratch_shapes=[pltpu.VMEM((TILE + 2 * SUBLANES, c), F32)],
        compiler_params=_cparams("parallel", "arbitrary"),
        name="gdn_prep",
    )(qkv, qkv, qkv, ab, sconv_w, alog_row, dtb_row)


def _gdn_kernel(q_ref, k_ref, v_ref, gc_ref, gr_ref, o_ref, s_ref):
    d = pl.program_id(0)
    n = q_ref.shape[1]
    nh = s_ref.shape[0]
    dk = s_ref.shape[1]

    @pl.when(pl.program_id(2) == 0)
    def _():
        s_ref[...] = jnp.zeros_like(s_ref)

    ii = lax.broadcasted_iota(jnp.int32, (n, n), 0)
    jj = lax.broadcasted_iota(jnp.int32, (n, n), 1)
    ti = jnp.where(d == 0, ii, n - 1 - ii)
    tj = jnp.where(d == 0, jj, n - 1 - jj)
    lower = ti >= tj
    strict = ti > tj

    def same_block(shift):
        return lax.shift_right_logical(ti, shift) == lax.shift_right_logical(tj, shift)

    gc = gc_ref[0, 0]
    gr = gr_ref[0, 0, 0]
    for hd in range(nh):
        sl = slice(hd * dk, (hd + 1) * dk)
        qh, kh, vh = q_ref[0, :, sl], k_ref[0, :, sl], v_ref[0, :, sl]
        g_col = gc[:, hd:hd + 1]
        beta = gc[:, nh + hd:nh + hd + 1]
        gtot_col = gc[:, 2 * nh + hd:2 * nh + hd + 1]
        g_row = gr[hd:hd + 1, :]
        gtot_row = gr[2 * nh + hd:2 * nh + hd + 1, :dk]
        decay = jnp.where(lower, jnp.exp(jnp.where(lower, g_col - g_row, 0.0)), 0.0)
        kb = kh * beta
        a_mat = jnp.where(strict, _dot_nt(kb, kh) * decay, 0.0)
        blk = same_block(3)
        dm = jnp.where(blk, a_mat, 0.0)
        f = -dm
        d2 = _dot(dm, dm)
        f = f + d2 + _dot(f, d2)
        d4 = _dot(d2, d2)
        f = f + d4 + _dot(f, d4)
        shift = 3
        while (1 << shift) < n:
            nblk = same_block(shift + 1)
            e = jnp.where(jnp.logical_and(nblk, jnp.logical_not(blk)), a_mat, 0.0)
            x = e + _dot(f, e)
            f = f - x - _dot(x, f)
            blk = nblk
            shift += 1
        eg = jnp.exp(g_col)
        rhs = jnp.concatenate([vh * beta, kb * eg], axis=-1)
        uw = rhs + _dot(f, rhs)
        u, w = uw[:, :dk], uw[:, dk:]
        s = s_ref[hd]
        v_new = u - _dot(w, s)
        qk = jnp.where(lower, _dot_nt(qh, kh) * decay, 0.0)
        o_ref[0, 0, :, sl] = _dot(qh * eg, s) + _dot(qk, v_new)
        k_tail = kh * jnp.exp(gtot_col - g_col)
        s_ref[hd] = s * jnp.exp(gtot_row) + _dot(k_tail.T, v_new)


def _scan_tile(d, s, nct, nt):
    rev = jnp.where(s < nct, nct - 1 - s, nt - 1 - (s - nct))
    return jnp.where(d == 0, s, rev)


def _gdn(q, k, v, gcol, grow, *, nct):
    bsz, lt, dim = q.shape
    nt = lt // TILE
    ng = gcol.shape[-1]
    tile = lambda d, b, s: _scan_tile(d, s, nct, nt)
    tok = pl.BlockSpec((1, TILE, dim), lambda d, b, s: (b, tile(d, b, s), 0))
    return pl.pallas_call(
        _gdn_kernel,
        out_shape=jax.ShapeDtypeStruct((2, bsz, lt, dim), F32),
        grid=(2, bsz, nt),
        in_specs=[tok, tok, tok,
                  pl.BlockSpec((1, 1, TILE, ng), lambda d, b, s: (d, b, tile(d, b, s), 0)),
                  pl.BlockSpec((1, 1, 1, ng, TILE), lambda d, b, s: (d, b, tile(d, b, s), 0, 0))],
        out_specs=pl.BlockSpec((1, 1, TILE, dim), lambda d, b, s: (d, b, tile(d, b, s), 0)),
        scratch_shapes=[pltpu.VMEM((GDN_HEADS, GDN_DK, dim // GDN_HEADS), F32)],
        compiler_params=_cparams("parallel", "parallel", "arbitrary"),
        name="gdn_scan",
    )(q, k, v, gcol, grow)


def _even_out_kernel(x_ref, mod_ref, conv_ref, of_ref, ob_ref, gate_ref, og_ref, wa_ref, wb_ref, o_ref,
                     *, nct, ctx_row):
    d = x_ref.shape[-1]
    row = jnp.where(pl.program_id(1) < nct, ctx_row, pl.program_id(0))
    o = of_ref[0, 0] + ob_ref[0, 0]
    dv = og_ref.shape[-1]
    heads = []
    for hd in range(o.shape[-1] // dv):
        p = o[:, hd * dv:(hd + 1) * dv]
        heads.append(p * lax.rsqrt(jnp.mean(p * p, axis=-1, keepdims=True) + EPS) * og_ref[...])
    gdn = jnp.concatenate(heads, axis=-1) * _silu(gate_ref[0])
    y = _dot(conv_ref[0], wa_ref[...]) + _dot(gdn, wb_ref[...])
    o_ref[0] = x_ref[0] + _mod_row(mod_ref, row, 5, d) * y


def _even_out(xs, mod, conv, o2, gate, onorm_g, wa, wb, *, nct):
    bsz, lt, d = xs.shape
    nt = lt // TILE
    dim = conv.shape[-1]
    tok = lambda n: pl.BlockSpec((1, TILE, n), lambda b, t: (b, t, 0))
    kern = functools.partial(_even_out_kernel, nct=nct, ctx_row=bsz)
    return pl.pallas_call(
        kern,
        out_shape=jax.ShapeDtypeStruct(xs.shape, F32),
        grid=(bsz, nt),
        in_specs=[tok(d), _full(mod.shape), tok(dim),
                  pl.BlockSpec((1, 1, TILE, dim), lambda b, t: (0, b, t, 0)),
                  pl.BlockSpec((1, 1, TILE, dim), lambda b, t: (1, b, t, 0)),
                  tok(dim), _full((1, onorm_g.shape[-1])), _full(wa.shape), _full(wb.shape)],
        out_specs=tok(d),
        input_output_aliases={0: 0},
        compiler_params=_cparams("parallel", "arbitrary"),
        name="even_out",
    )(xs, mod, conv, o2, o2, gate, onorm_g.reshape(1, -1), wa, wb)


def _s5_in_kernel(x_ref, mod_ref, g_ref, w_ref, u_ref, *, nct, ctx_row):
    d = x_ref.shape[-1]
    row = jnp.where(pl.program_id(1) < nct, ctx_row, pl.program_id(0))
    h = _modulated_norm(x_ref[0], g_ref[...], _mod_row(mod_ref, row, 3, d), _mod_row(mod_ref, row, 4, d))
    u_ref[...] = _dot(h, w_ref[...])


def _s5_in(xs, mod, g, w_in, *, nct):
    bsz, lt, d = xs.shape
    nt = lt // TILE
    sd = w_in.shape[-1]
    kern = functools.partial(_s5_in_kernel, nct=nct, ctx_row=bsz)
    return pl.pallas_call(
        kern,
        out_shape=jax.ShapeDtypeStruct((lt, bsz * sd), F32),
        grid=(bsz, nt),
        in_specs=[pl.BlockSpec((1, TILE, d), lambda b, t: (b, t, 0)), _full(mod.shape), _full((1, d)),
                  _full(w_in.shape)],
        out_specs=pl.BlockSpec((TILE, sd), lambda b, t: (t, b)),
        compiler_params=_cparams("parallel", "arbitrary"),
        name="s5_in",
    )(xs, mod, g.reshape(1, d), w_in)


def _s5_scan_kernel(u_ref, bre_ref, bim_ref, are_ref, aim_ref, cre_ref, cim_ref, y_ref,
                    xr_ref, xi_ref, sr_ref, si_ref, *, steps, bsz):
    d = pl.program_id(0)
    nblk = bre_ref.shape[1]
    kin = bre_ref.shape[2]
    kst = bre_ref.shape[3]

    @pl.when(pl.program_id(1) == 0)
    def _():
        sr_ref[...] = jnp.zeros_like(sr_ref)
        si_ref[...] = jnp.zeros_like(si_ref)

    u = u_ref[...].astype(BF16)
    for j in range(nblk):
        uj = u[:, j * kin:(j + 1) * kin]
        xr_ref[:, j * kst:(j + 1) * kst] = jnp.dot(uj, bre_ref[0, j], preferred_element_type=F32)
        xi_ref[:, j * kst:(j + 1) * kst] = jnp.dot(uj, bim_ref[0, j], preferred_element_type=F32)

    nstate = xr_ref.shape[1]
    half = nstate // 2
    for part in range(2):
        ls = slice(part * half, (part + 1) * half)
        ar = are_ref[0, :, ls]
        ai = aim_ref[0, :, ls]

        def body(i, carry):
            xr, xi = carry
            t = jnp.where(d == 0, i, steps - 1 - i)
            r0 = pl.multiple_of(t * bsz, bsz)
            br = xr_ref[pl.ds(r0, bsz), ls]
            bi = xi_ref[pl.ds(r0, bsz), ls]
            nr = ar * xr - ai * xi + br
            ni = ar * xi + ai * xr + bi
            xr_ref[pl.ds(r0, bsz), ls] = nr
            xi_ref[pl.ds(r0, bsz), ls] = ni
            return nr, ni

        xr, xi = lax.fori_loop(0, steps, body, (sr_ref[:, ls], si_ref[:, ls]))
        sr_ref[:, ls] = xr
        si_ref[:, ls] = xi

    kout = cre_ref.shape[3]
    xrb = xr_ref[...].astype(BF16)
    xib = xi_ref[...].astype(BF16)
    for j in range(nblk):
        ks = slice(j * kst, (j + 1) * kst)
        y_ref[0, :, j * kout:(j + 1) * kout] = (
            jnp.dot(xrb[:, ks], cre_ref[0, j], preferred_element_type=F32)
            - jnp.dot(xib[:, ks], cim_ref[0, j], preferred_element_type=F32))


def _s5_scan(u2, bre, bim, are, aim, cre, cim, *, bsz, nct_steps):
    rows, sd = u2.shape
    tr = S5_STEPS * bsz
    ns = rows // tr
    nstate = are.shape[-1]
    tile = lambda d, s: _scan_tile(d, s, nct_steps, ns)
    par = lambda a: pl.BlockSpec((1,) + a.shape[1:], lambda d, s: (d,) + (0,) * (a.ndim - 1))
    kern = functools.partial(_s5_scan_kernel, steps=S5_STEPS, bsz=bsz)
    return pl.pallas_call(
        kern,
        out_shape=jax.ShapeDtypeStruct((2, rows, sd), F32),
        grid=(2, ns),
        in_specs=[pl.BlockSpec((tr, sd), lambda d, s: (tile(d, s), 0)),
                  par(bre), par(bim), par(are), par(aim), par(cre), par(cim)],
        out_specs=pl.BlockSpec((1, tr, sd), lambda d, s: (d, tile(d, s), 0)),
        scratch_shapes=[pltpu.VMEM((tr, nstate), F32), pltpu.VMEM((tr, nstate), F32),
                        pltpu.VMEM((bsz, nstate), F32), pltpu.VMEM((bsz, nstate), F32)],
        compiler_params=_cparams("parallel", "arbitrary"),
        name="s5_scan",
    )(u2, bre, bim, are, aim, cre, cim)


def _s5_out_kernel(x_ref, mod_ref, u_ref, yf_ref, yb_ref, dsk_ref, wa_ref, wg_ref, o_ref, *, nct, ctx_row):
    d = x_ref.shape[-1]
    row = jnp.where(pl.program_id(1) < nct, ctx_row, pl.program_id(0))
    y = dsk_ref[...] * u_ref[...] + yf_ref[0] + yb_ref[0]
    z = 0.5 * y * (1.0 + jnp.tanh(math.sqrt(2.0 / math.pi) * (y + 0.044715 * (y * y * y))))
    zb = z.astype(BF16)
    a = jnp.dot(zb, wa_ref[...], preferred_element_type=F32)
    gt = jnp.dot(zb, wg_ref[...], preferred_element_type=F32)
    o_ref[0] = x_ref[0] + _mod_row(mod_ref, row, 5, d) * (a * _sigmoid(gt))


def _s5_out(xs, mod, u_tm, y_tm, d_skip, wa, wg, *, nct):
    bsz, lt, d = xs.shape
    nt = lt // TILE
    sd = d_skip.shape[-1]
    kern = functools.partial(_s5_out_kernel, nct=nct, ctx_row=bsz)
    return pl.pallas_call(
        kern,
        out_shape=jax.ShapeDtypeStruct(xs.shape, F32),
        grid=(bsz, nt),
        in_specs=[pl.BlockSpec((1, TILE, d), lambda b, t: (b, t, 0)), _full(mod.shape),
                  pl.BlockSpec((TILE, sd), lambda b, t: (t, b)),
                  pl.BlockSpec((1, TILE, sd), lambda b, t: (0, t, b)),
                  pl.BlockSpec((1, TILE, sd), lambda b, t: (1, t, b)),
                  _full((1, sd)), _full(wa.shape), _full(wg.shape)],
        out_specs=pl.BlockSpec((1, TILE, d), lambda b, t: (b, t, 0)),
        input_output_aliases={0: 0},
        compiler_params=_cparams("parallel", "arbitrary"),
        name="s5_out",
    )(xs, mod, u_tm, y_tm, y_tm, d_skip.reshape(1, sd), wa, wg)


def _s5_params(lam_re, lam_im, log_step, b_re, b_im, c_re, c_im, bsz):
    lam_re = jnp.minimum(lam_re, -1e-4)
    dt = jnp.exp(log_step)[..., None]
    mag = jnp.exp(lam_re * dt)
    ab_re = mag * jnp.cos(lam_im * dt)
    ab_im = mag * jnp.sin(lam_im * dt)
    den = lam_re * lam_re + lam_im * lam_im
    num_re = ab_re - 1.0
    coef_re = (num_re * lam_re + ab_im * lam_im) / den
    coef_im = (ab_im * lam_re - num_re * lam_im) / den
    bb_re = coef_re[..., None] * b_re - coef_im[..., None] * b_im
    bb_im = coef_re[..., None] * b_im + coef_im[..., None] * b_re
    ndir, g, p, h = bb_re.shape
    gb = S5_BLOCK_GROUPS
    nb = g // gb
    eye = jnp.eye(gb, dtype=F32)

    def in_blocks(bb):
        bb = bb.reshape(ndir, nb, gb, p, h)
        return jnp.einsum('djaph,ab->djahbp', bb, eye).reshape(ndir, nb, gb * h, gb * p).astype(BF16)

    def out_blocks(cc):
        cc = cc.reshape(ndir, nb, gb, h, p)
        return jnp.einsum('djahp,ab->djapbh', cc, eye).reshape(ndir, nb, gb * p, gb * h).astype(BF16)

    bcast = lambda a: jnp.broadcast_to(a.reshape(ndir, 1, g * p), (ndir, bsz, g * p))
    return (in_blocks(bb_re), in_blocks(bb_im), bcast(ab_re), bcast(ab_im),
            out_blocks(c_re), out_blocks(c_im))


def kernel(x, c, ctx, c_ctx, w_mod, b_mod, norm_g, ffn_w13, ffn_w2, final_g, e_w_in, e_conv_w, e_conv_b, e_cln_g, e_cln_b, e_sconv_w, e_a_log, e_dt_bias, e_onorm_g, e_w_out, o_w_in, o_lam_re, o_lam_im, o_log_step, o_b_re, o_b_im, o_c_re, o_c_im, o_d, o_w_out):
    bsz, seq, d = x.shape
    lc = ctx.shape[1]
    depth = w_mod.shape[0]
    assert lc % TILE == 0 and seq % TILE == 0 and bsz == SUBLANES and bsz < MOD_ROWS
    nct = lc // TILE
    lt = lc + seq
    nt = lt // TILE
    f = ffn_w2.shape[2]
    cdim = e_conv_w.shape[-1]
    gdim = e_w_out.shape[1] - cdim
    nh = gdim // GDN_DK
    n_ab = e_w_in.shape[-1] - 2 * cdim - 4 * gdim

    cc = jnp.zeros((MOD_ROWS, d), F32).at[:bsz].set(c).at[bsz].set(c_ctx)
    mod = _mod_table(cc, w_mod, b_mod)
    xs = jnp.concatenate([ctx, x], axis=1)

    w13 = ffn_w13.astype(BF16)
    w2 = ffn_w2.astype(BF16)

    for l in range(depth):
        last = l == depth - 1
        j = l // 2
        ml = mod[l]
        xs = _ffn(xs, ml, norm_g[l, 0], w13[l, 0, :, :f], w13[l, 0, :, f:], w2[l, 0], final_g,
                  nct=nct, k0=0, final=False)
        if l % 2 == 0:
            w_in = e_w_in[j].astype(BF16)
            c0, c1, c2 = 2 * cdim, 2 * cdim + 3 * gdim, 2 * cdim + 4 * gdim
            wab = jnp.zeros((d, LANES), BF16).at[:, :n_ab].set(w_in[:, c2:])
            conv, qkv, gate, ab = _even_in(xs, ml, norm_g[l, 1], w_in[:, :c0], w_in[:, c0:c1], w_in[:, c1:c2], wab,
                                           e_conv_w[j], e_conv_b[j], e_cln_g[j], e_cln_b[j], nct=nct, n_ab=n_ab)
            alog_row = jnp.zeros((2, 2, nh), F32).at[:, 0].set(e_a_log[j]).reshape(1, n_ab)
            dtb_row = jnp.zeros((2, 2, nh), F32).at[:, 0].set(e_dt_bias[j]).reshape(1, n_ab)
            q, k, v, gf, gb, gt, beta = _gdn_prep(qkv, ab, e_sconv_w[j], alog_row, dtb_row, nct=nct)
            pick = lambda a, dr, jj: a.reshape(bsz, lt, 2, 2, nh)[:, :, dr, jj]
            gcol = jnp.stack([jnp.concatenate([pick(gf, 0, 0), pick(beta, 0, 1), pick(gt, 0, 0)], axis=-1),
                              jnp.concatenate([pick(gb, 1, 0), pick(beta, 1, 1), pick(gt, 1, 0)], axis=-1)])
            grow = jnp.swapaxes(gcol.reshape(2, bsz, nt, TILE, 3 * nh), -1, -2)
            o2 = _gdn(q, k, v, gcol, grow, nct=nct)
            w_out = e_w_out[j].astype(BF16)
            xs = _even_out(xs, ml, conv, o2, gate, jnp.tile(e_onorm_g[j], 1), w_out[:cdim], w_out[cdim:], nct=nct)
        else:
            u_tm = _s5_in(xs, ml, norm_g[l, 1], o_w_in[j].astype(BF16), nct=nct)
            sd = u_tm.shape[1] // bsz
            params = _s5_params(o_lam_re[j], o_lam_im[j], o_log_step[j], o_b_re[j], o_b_im[j],
                                o_c_re[j], o_c_im[j], bsz)
            y2 = _s5_scan(u_tm.reshape(lt * bsz, sd), *params, bsz=bsz, nct_steps=lc // S5_STEPS)
            w_o = o_w_out[j].astype(BF16)
            xs = _s5_out(xs, ml, u_tm, y2.reshape(2, lt, bsz * sd), o_d[j], w_o[:, :d], w_o[:, d:], nct=nct)
        xs = _ffn(xs, ml, norm_g[l, 2], w13[l, 1, :, :f], w13[l, 1, :, f:], w2[l, 1], final_g,
                  nct=nct, k0=6, final=last)
    return xs
```

```python
import functools
import math

import jax
import jax.numpy as jnp
from jax import lax
from jax.experimental import pallas as pl
from jax.experimental.pallas import tpu as pltpu

EPS = 1e-6
TILE = 256
LANES = 128
SUBLANES = 8
GRID_W = 64
GDN_HEADS = 4
GDN_DK = 128
S5_GROUP_CH = 16
S5_STATE = 64
S5_BLOCK_GROUPS = 8
S5_STEPS = 64
MOD_ROWS = 16
VMEM_LIMIT = 56 * 1024 * 1024

F32 = jnp.float32
BF16 = jnp.bfloat16


def _cparams(*sem):
    return pltpu.CompilerParams(dimension_semantics=sem, vmem_limit_bytes=VMEM_LIMIT)


def _dot(a, b):
    return jnp.dot(a.astype(BF16), b.astype(BF16), preferred_element_type=F32)


def _dot_nt(a, b):
    return lax.dot_general(a.astype(BF16), b.astype(BF16), (((1,), (1,)), ((), ())),
                           preferred_element_type=F32)


def _dot_exact(a, b):
    return jnp.dot(a, b, precision=lax.Precision.HIGHEST, preferred_element_type=F32)


def _sigmoid(x):
    return 1.0 / (1.0 + jnp.exp(-x))


def _silu(x):
    return x * _sigmoid(x)


def _mod_row(mod_ref, row, k, d):
    return mod_ref[pl.ds(row, 1), pl.ds(k * d, d)]


def _modulated_norm(x, g, shift, scale):
    y = x * lax.rsqrt(jnp.mean(x * x, axis=-1, keepdims=True) + EPS)
    return (y * g) * (1.0 + scale) + shift


def _full(shape):
    n = len(shape)
    return pl.BlockSpec(shape, lambda *_: (0,) * n)


def _mod_kernel(c_ref, w_ref, b_ref, o_ref):
    h = _silu(c_ref[...])
    o_ref[0] = _dot(h, w_ref[0]) + b_ref[0]


def _mod_table(cc, w_mod, b_mod):
    depth, d, nd = w_mod.shape
    tn = 1024
    return pl.pallas_call(
        _mod_kernel,
        out_shape=jax.ShapeDtypeStruct((depth, MOD_ROWS, nd), F32),
        grid=(depth, nd // tn),
        in_specs=[pl.BlockSpec((MOD_ROWS, d), lambda l, j: (0, 0)),
                  pl.BlockSpec((1, d, tn), lambda l, j: (l, 0, j)),
                  pl.BlockSpec((1, 1, tn), lambda l, j: (l, 0, j))],
        out_specs=pl.BlockSpec((1, MOD_ROWS, tn), lambda l, j: (l, 0, j)),
        compiler_params=_cparams("parallel", "parallel"),
        name="mod_table",
    )(cc, w_mod, b_mod.reshape(depth, 1, nd))


def _ffn_kernel(x_ref, mod_ref, g_ref, w13_ref, w2_ref, fg_ref, o_ref, t_ref, *,
                nct, ctx_row, k0, fchunk, final):
    d = x_ref.shape[-1]
    f = w2_ref.shape[0]
    row = jnp.where(pl.program_id(1) < nct, ctx_row, pl.program_id(0))
    x = x_ref[0]
    h = _modulated_norm(x, g_ref[...], _mod_row(mod_ref, row, k0, d), _mod_row(mod_ref, row, k0 + 1, d))
    hb = h.astype(BF16)
    for c in range(f // fchunk):
        sl = slice(c * fchunk, (c + 1) * fchunk)
        a = jnp.dot(hb, w13_ref[:, sl], preferred_element_type=F32)
        b = jnp.dot(hb, w13_ref[:, f + c * fchunk:f + (c + 1) * fchunk], preferred_element_type=F32)
        t_ref[:, sl] = (_silu(a) * b).astype(BF16)
    y = jnp.dot(t_ref[...], w2_ref[...], preferred_element_type=F32)
    out = x + (0.5 * _mod_row(mod_ref, row, k0 + 2, d)) * y
    if final:
        out = out * lax.rsqrt(jnp.mean(out * out, axis=-1, keepdims=True) + EPS) * fg_ref[...]
    o_ref[0] = out


def _ffn(xs, mod, g, w13, w2, final_g, *, nct, k0, final):
    bsz, lt, d = xs.shape
    f = w2.shape[0]
    nt = lt // TILE
    kern = functools.partial(_ffn_kernel, nct=nct, ctx_row=bsz, k0=k0, fchunk=256, final=final)
    if final:
        out_shape = jax.ShapeDtypeStruct((bsz, lt - nct * TILE, d), F32)
        out_spec = pl.BlockSpec((1, TILE, d), lambda b, t: (b, jnp.maximum(t - nct, 0), 0))
        aliases = {}
    else:
        out_shape = jax.ShapeDtypeStruct(xs.shape, F32)
        out_spec = pl.BlockSpec((1, TILE, d), lambda b, t: (b, t, 0))
        aliases = {0: 0}
    return pl.pallas_call(
        kern,
        out_shape=out_shape,
        grid=(bsz, nt),
        in_specs=[pl.BlockSpec((1, TILE, d), lambda b, t: (b, t, 0)),
                  _full(mod.shape), _full((1, d)),
                  _full((d, 2 * f)), _full((f, d)), _full((1, d))],
        out_specs=out_spec,
        scratch_shapes=[pltpu.VMEM((TILE, f), BF16)],
        input_output_aliases=aliases,
        compiler_params=_cparams("parallel", "arbitrary"),
        name="ffn_final" if final else "ffn",
    )(xs, mod, g.reshape(1, d), w13, w2, final_g.reshape(1, d))


def _depthwise_rows(h, pad_ref, w_ref, seg, joined):
    n, _ = h.shape
    kw = w_ref.shape[0]
    half = kw // 2
    lead = -(-half // SUBLANES) * SUBLANES
    stride = seg + 2 * lead
    nseg = n // seg
    rows = nseg * stride
    zeros = jnp.zeros((lead, h.shape[1]), F32)
    for s in range(nseg):
        base = s * stride
        before = jnp.where(joined, h[s * seg - lead:s * seg], zeros) if s > 0 else zeros
        after = jnp.where(joined, h[(s + 1) * seg:(s + 1) * seg + lead], zeros) if s < nseg - 1 else zeros
        pad_ref[0, base:base + lead, :] = before
        pad_ref[0, base + lead:base + lead + seg, :] = h[s * seg:(s + 1) * seg]
        pad_ref[0, base + lead + seg:base + stride, :] = after
    for r in range(1, SUBLANES):
        pad_ref[r, 0:rows - SUBLANES, :] = pad_ref[0, r:rows - SUBLANES + r, :]
    outs = []
    for s in range(nseg):
        acc = None
        for k in range(kw):
            a, r = divmod(s * stride + lead - half + k, SUBLANES)
            term = w_ref[k:k + 1, :] * pad_ref[r, a * SUBLANES:a * SUBLANES + seg, :]
            acc = term if acc is None else acc + term
        outs.append(acc)
    return outs[0] if len(outs) == 1 else jnp.concatenate(outs, axis=0)


def _even_in_kernel(x_ref, mod_ref, g_ref, wglu_ref, wqkv_ref, wgate_ref, wab_ref,
                    cw_ref, cb_ref, lg_ref, lb_ref,
                    conv_ref, qkv_ref, gate_ref, ab_ref, pad_ref, *, nct, ctx_row):
    d = x_ref.shape[-1]
    cdim = cw_ref.shape[-1]
    is_ctx = pl.program_id(1) < nct
    row = jnp.where(is_ctx, ctx_row, pl.program_id(0))
    h = _modulated_norm(x_ref[0], g_ref[...], _mod_row(mod_ref, row, 3, d), _mod_row(mod_ref, row, 4, d))
    hb = h.astype(BF16)
    qkv_ref[0] = jnp.dot(hb, wqkv_ref[...], preferred_element_type=F32)
    gate_ref[0] = jnp.dot(hb, wgate_ref[...], preferred_element_type=F32)
    ab_ref[0] = jnp.dot(hb, wab_ref[...], preferred_element_type=F32)[:, :ab_ref.shape[-1]]
    glu = jnp.dot(hb, wglu_ref[...], preferred_element_type=F32)
    hc = glu[:, :cdim] * _sigmoid(glu[:, cdim:])
    y = _depthwise_rows(hc, pad_ref, cw_ref, GRID_W, is_ctx) + cb_ref[...]
    yc = y - jnp.mean(y, axis=-1, keepdims=True)
    yn = yc * lax.rsqrt(jnp.mean(yc * yc, axis=-1, keepdims=True) + EPS)
    conv_ref[0] = _silu(yn * lg_ref[...] + lb_ref[...])


def _even_in(xs, mod, g, wglu, wqkv, wgate, wab, conv_w, conv_b, cln_g, cln_b, *, nct, n_ab):
    bsz, lt, d = xs.shape
    nt = lt // TILE
    cdim = conv_w.shape[-1]
    kw = conv_w.shape[0]
    lead = -(-(kw // 2) // SUBLANES) * SUBLANES
    pad_rows = (TILE // GRID_W) * (GRID_W + 2 * lead)
    tok = lambda n: pl.BlockSpec((1, TILE, n), lambda b, t: (b, t, 0))
    kern = functools.partial(_even_in_kernel, nct=nct, ctx_row=bsz)
    return pl.pallas_call(
        kern,
        out_shape=[jax.ShapeDtypeStruct((bsz, lt, cdim), F32),
                   jax.ShapeDtypeStruct((bsz, lt, wqkv.shape[-1]), F32),
                   jax.ShapeDtypeStruct((bsz, lt, wgate.shape[-1]), F32),
                   jax.ShapeDtypeStruct((bsz, lt, n_ab), F32)],
        grid=(bsz, nt),
        in_specs=[tok(d), _full(mod.shape), _full((1, d)),
                  _full(wglu.shape), _full(wqkv.shape), _full(wgate.shape), _full(wab.shape),
                  _full(conv_w.shape), _full((1, cdim)), _full((1, cdim)), _full((1, cdim))],
        out_specs=[tok(cdim), tok(wqkv.shape[-1]), tok(wgate.shape[-1]), tok(n_ab)],
        scratch_shapes=[pltpu.VMEM((SUBLANES, pad_rows, cdim), F32)],
        compiler_params=_cparams("parallel", "arbitrary"),
        name="even_in",
    )(xs, mod, g.reshape(1, d), wglu, wqkv, wgate, wab,
      conv_w, conv_b.reshape(1, cdim), cln_g.reshape(1, cdim), cln_b.reshape(1, cdim))


def _gdn_prep_kernel(qkv_ref, prev_ref, next_ref, ab_ref, sw_ref, alog_ref, dtb_ref,
                     q_ref, k_ref, v_ref, gf_ref, gb_ref, gt_ref, beta_ref, pad_ref, *, nct, nt):
    t = pl.program_id(1)
    n = qkv_ref.shape[1]
    c = qkv_ref.shape[2]
    kw = sw_ref.shape[0]
    half = kw // 2
    has_prev = jnp.logical_and(t != 0, t != nct)
    has_next = jnp.logical_and(t != nct - 1, t != nt - 1)
    pad_ref[0:SUBLANES, :] = jnp.where(has_prev, prev_ref[0], 0.0)
    pad_ref[SUBLANES:SUBLANES + n, :] = qkv_ref[0]
    pad_ref[SUBLANES + n:2 * SUBLANES + n, :] = jnp.where(has_next, next_ref[0], 0.0)
    base = SUBLANES - half
    acc = sw_ref[0:1, :] * pad_ref[base:base + n, :]
    for k in range(1, kw):
        acc = acc + sw_ref[k:k + 1, :] * pad_ref[base + k:base + k + n, :]
    y = _silu(acc)
    dim = c // 3
    for i, ref in enumerate((q_ref, k_ref, v_ref)):
        part = y[:, i * dim:(i + 1) * dim]
        if i == 2:
            ref[0] = part
            continue
        heads = []
        for hd in range(dim // GDN_DK):
            p = part[:, hd * GDN_DK:(hd + 1) * GDN_DK]
            p = p * lax.rsqrt(jnp.sum(p * p, axis=-1, keepdims=True) + EPS)
            heads.append(p * (GDN_DK ** -0.5) if i == 0 else p)
        ref[0] = jnp.concatenate(heads, axis=-1)
    ab = ab_ref[0]
    z = ab + dtb_ref[...]
    softplus = jnp.maximum(z, 0.0) + jnp.log1p(jnp.exp(-jnp.abs(z)))
    la = -jnp.exp(alog_ref[...]) * softplus
    ii = lax.broadcasted_iota(jnp.int32, (n, n), 0)
    jj = lax.broadcasted_iota(jnp.int32, (n, n), 1)
    gf_ref[0] = _dot_exact(jnp.where(ii >= jj, 1.0, 0.0), la)
    gb_ref[0] = _dot_exact(jnp.where(ii <= jj, 1.0, 0.0), la)
    gt_ref[0] = _dot_exact(jnp.ones((n, n), F32), la)
    beta_ref[0] = _sigmoid(ab)


def _gdn_prep(qkv, ab, sconv_w, alog_row, dtb_row, *, nct):
    bsz, lt, c = qkv.shape
    nt = lt // TILE
    dim = c // 3
    n_ab = ab.shape[-1]
    per = TILE // SUBLANES
    nblk = lt // SUBLANES
    tok = lambda n: pl.BlockSpec((1, TILE, n), lambda b, t: (b, t, 0))
    kern = functools.partial(_gdn_prep_kernel, nct=nct, nt=nt)
    return pl.pallas_call(
        kern,
        out_shape=[jax.ShapeDtypeStruct((bsz, lt, dim), F32)] * 3
        + [jax.ShapeDtypeStruct((bsz, lt, n_ab), F32)] * 4,
        grid=(bsz, nt),
        in_specs=[tok(c),
                  pl.BlockSpec((1, SUBLANES, c), lambda b, t: (b, jnp.maximum(t * per - 1, 0), 0)),
                  pl.BlockSpec((1, SUBLANES, c), lambda b, t: (b, jnp.minimum((t + 1) * per, nblk - 1), 0)),
                  tok(n_ab), _full(sconv_w.shape), _full((1, n_ab)), _full((1, n_ab))],
        out_specs=[tok(dim)] * 3 + [tok(n_ab)] * 4,
        scratch_shapes=[pltpu.VMEM((TILE + 2 * SUBLANES, c), F32)],
        compiler_params=_cparams("parallel", "arbitrary"),
        name="gdn_prep",
    )(qkv, qkv, qkv, ab, sconv_w, alog_row, dtb_row)


def _gdn_kernel(q_ref, k_ref, v_ref, gc_ref, gr_ref, o_ref, s_ref):
    d = pl.program_id(0)
    n = q_ref.shape[1]
    nh = s_ref.shape[0]
    dk = s_ref.shape[1]

    @pl.when(pl.program_id(2) == 0)
    def _():
        s_ref[...] = jnp.zeros_like(s_ref)

    ii = lax.broadcasted_iota(jnp.int32, (n, n), 0)
    jj = lax.broadcasted_iota(jnp.int32, (n, n), 1)
    ti = jnp.where(d == 0, ii, n - 1 - ii)
    tj = jnp.where(d == 0, jj, n - 1 - jj)
    lower = ti >= tj
    strict = ti > tj

    def same_block(shift):
        return lax.shift_right_logical(ti, shift) == lax.shift_right_logical(tj, shift)

    gc = gc_ref[0, 0]
    gr = gr_ref[0, 0, 0]
    heads = range(nh)
    sls = [slice(hd * dk, (hd + 1) * dk) for hd in heads]
    qs = [q_ref[0, :, sl] for sl in sls]
    ks = [k_ref[0, :, sl] for sl in sls]
    vs = [v_ref[0, :, sl] for sl in sls]
    g_col = [gc[:, hd:hd + 1] for hd in heads]
    beta = [gc[:, nh + hd:nh + hd + 1] for hd in heads]
    gtot_col = [gc[:, 2 * nh + hd:2 * nh + hd + 1] for hd in heads]
    g_row = [gr[hd:hd + 1, :] for hd in heads]
    gtot_row = [gr[2 * nh + hd:2 * nh + hd + 1, :dk] for hd in heads]
    decay = [jnp.where(lower, jnp.exp(jnp.where(lower, g_col[h] - g_row[h], 0.0)), 0.0) for h in heads]
    kb = [ks[h] * beta[h] for h in heads]
    a_mat = [jnp.where(strict, _dot_nt(kb[h], ks[h]) * decay[h], 0.0) for h in heads]
    blk = same_block(3)
    dm = [jnp.where(blk, a_mat[h], 0.0) for h in heads]
    d2 = [_dot(dm[h], dm[h]) for h in heads]
    f = [d2[h] - dm[h] - _dot(dm[h], d2[h]) for h in heads]
    d4 = [_dot(d2[h], d2[h]) for h in heads]
    f = [f[h] + d4[h] + _dot(f[h], d4[h]) for h in heads]
    shift = 3
    while (1 << shift) < n:
        nblk = same_block(shift + 1)
        sel = jnp.logical_and(nblk, jnp.logical_not(blk))
        e = [jnp.where(sel, a_mat[h], 0.0) for h in heads]
        x = [e[h] + _dot(f[h], e[h]) for h in heads]
        f = [f[h] - x[h] - _dot(x[h], f[h]) for h in heads]
        blk = nblk
        shift += 1
    eg = [jnp.exp(g_col[h]) for h in heads]
    rhs = [jnp.concatenate([vs[h] * beta[h], kb[h] * eg[h]], axis=-1) for h in heads]
    uw = [rhs[h] + _dot(f[h], rhs[h]) for h in heads]
    s = [s_ref[h] for h in heads]
    v_new = [uw[h][:, :dk] - _dot(uw[h][:, dk:], s[h]) for h in heads]
    qk = [jnp.where(lower, _dot_nt(qs[h], ks[h]) * decay[h], 0.0) for h in heads]
    for h in heads:
        o_ref[0, 0, :, sls[h]] = _dot(qs[h] * eg[h], s[h]) + _dot(qk[h], v_new[h])
    for h in heads:
        k_tail = ks[h] * jnp.exp(gtot_col[h] - g_col[h])
        s_ref[h] = s[h] * jnp.exp(gtot_row[h]) + _dot(k_tail.T, v_new[h])


def _scan_tile(d, s, nct, nt):
    rev = jnp.where(s < nct, nct - 1 - s, nt - 1 - (s - nct))
    return jnp.where(d == 0, s, rev)


def _gdn(q, k, v, gcol, grow, *, nct):
    bsz, lt, dim = q.shape
    nt = lt // TILE
    ng = gcol.shape[-1]
    tile = lambda d, b, s: _scan_tile(d, s, nct, nt)
    tok = pl.BlockSpec((1, TILE, dim), lambda d, b, s: (b, tile(d, b, s), 0))
    return pl.pallas_call(
        _gdn_kernel,
        out_shape=jax.ShapeDtypeStruct((2, bsz, lt, dim), F32),
        grid=(2, bsz, nt),
        in_specs=[tok, tok, tok,
                  pl.BlockSpec((1, 1, TILE, ng), lambda d, b, s: (d, b, tile(d, b, s), 0)),
                  pl.BlockSpec((1, 1, 1, ng, TILE), lambda d, b, s: (d, b, tile(d, b, s), 0, 0))],
        out_specs=pl.BlockSpec((1, 1, TILE, dim), lambda d, b, s: (d, b, tile(d, b, s), 0)),
        scratch_shapes=[pltpu.VMEM((GDN_HEADS, GDN_DK, dim // GDN_HEADS), F32)],
        compiler_params=_cparams("parallel", "parallel", "arbitrary"),
        name="gdn_scan",
    )(q, k, v, gcol, grow)


def _even_out_kernel(x_ref, mod_ref, conv_ref, of_ref, ob_ref, gate_ref, og_ref, wa_ref, wb_ref, o_ref,
                     *, nct, ctx_row):
    d = x_ref.shape[-1]
    row = jnp.where(pl.program_id(1) < nct, ctx_row, pl.program_id(0))
    o = of_ref[0, 0] + ob_ref[0, 0]
    dv = og_ref.shape[-1]
    heads = []
    for hd in range(o.shape[-1] // dv):
        p = o[:, hd * dv:(hd + 1) * dv]
        heads.append(p * lax.rsqrt(jnp.mean(p * p, axis=-1, keepdims=True) + EPS) * og_ref[...])
    gdn = jnp.concatenate(heads, axis=-1) * _silu(gate_ref[0])
    y = _dot(conv_ref[0], wa_ref[...]) + _dot(gdn, wb_ref[...])
    o_ref[0] = x_ref[0] + _mod_row(mod_ref, row, 5, d) * y


def _even_out(xs, mod, conv, o2, gate, onorm_g, wa, wb, *, nct):
    bsz, lt, d = xs.shape
    nt = lt // TILE
    dim = conv.shape[-1]
    tok = lambda n: pl.BlockSpec((1, TILE, n), lambda b, t: (b, t, 0))
    kern = functools.partial(_even_out_kernel, nct=nct, ctx_row=bsz)
    return pl.pallas_call(
        kern,
        out_shape=jax.ShapeDtypeStruct(xs.shape, F32),
        grid=(bsz, nt),
        in_specs=[tok(d), _full(mod.shape), tok(dim),
                  pl.BlockSpec((1, 1, TILE, dim), lambda b, t: (0, b, t, 0)),
                  pl.BlockSpec((1, 1, TILE, dim), lambda b, t: (1, b, t, 0)),
                  tok(dim), _full((1, onorm_g.shape[-1])), _full(wa.shape), _full(wb.shape)],
        out_specs=tok(d),
        input_output_aliases={0: 0},
        compiler_params=_cparams("parallel", "arbitrary"),
        name="even_out",
    )(xs, mod, conv, o2, o2, gate, onorm_g.reshape(1, -1), wa, wb)


def _s5_in_kernel(x_ref, mod_ref, g_ref, w_ref, u_ref, *, nct, ctx_row):
    d = x_ref.shape[-1]
    n = x_ref.shape[1]
    b = pl.program_id(1)
    row = jnp.where(pl.program_id(0) < nct, ctx_row, b)
    h = _modulated_norm(x_ref[0], g_ref[...], _mod_row(mod_ref, row, 3, d), _mod_row(mod_ref, row, 4, d))
    u = _dot(h, w_ref[...])
    for c in range(u_ref.shape[0]):
        u_ref[c, pl.ds(b, n, stride=ctx_row), :] = u[:, c * LANES:(c + 1) * LANES]


def _s5_in(xs, mod, g, w_in, *, nct):
    bsz, lt, d = xs.shape
    nt = lt // TILE
    sd = w_in.shape[-1]
    kern = functools.partial(_s5_in_kernel, nct=nct, ctx_row=bsz)
    return pl.pallas_call(
        kern,
        out_shape=jax.ShapeDtypeStruct((sd // LANES, lt * bsz, LANES), F32),
        grid=(nt, bsz),
        in_specs=[pl.BlockSpec((1, TILE, d), lambda t, b: (b, t, 0)), _full(mod.shape), _full((1, d)),
                  _full(w_in.shape)],
        out_specs=pl.BlockSpec((sd // LANES, TILE * bsz, LANES), lambda t, b: (0, t, 0)),
        compiler_params=_cparams("parallel", "arbitrary"),
        name="s5_in",
    )(xs, mod, g.reshape(1, d), w_in)


def _s5_scan_kernel(u_ref, bre_ref, bim_ref, are_ref, aim_ref, cre_ref, cim_ref, y_ref,
                    xr_ref, xi_ref, sr_ref, si_ref, *, steps, bsz):
    d = pl.program_id(0)
    nblk = bre_ref.shape[1]
    kst = bre_ref.shape[3]

    @pl.when(pl.program_id(1) == 0)
    def _():
        sr_ref[...] = jnp.zeros_like(sr_ref)
        si_ref[...] = jnp.zeros_like(si_ref)

    for j in range(nblk):
        uj = u_ref[j].astype(BF16)
        xr_ref[:, j * kst:(j + 1) * kst] = jnp.dot(uj, bre_ref[0, j], preferred_element_type=F32)
        xi_ref[:, j * kst:(j + 1) * kst] = jnp.dot(uj, bim_ref[0, j], preferred_element_type=F32)

    nstate = xr_ref.shape[1]
    half = nstate // 2
    for part in range(2):
        ls = slice(part * half, (part + 1) * half)
        ar = are_ref[0, :, ls]
        ai = aim_ref[0, :, ls]

        def body(i, carry):
            xr, xi = carry
            t = jnp.where(d == 0, i, steps - 1 - i)
            r0 = pl.multiple_of(t * bsz, bsz)
            br = xr_ref[pl.ds(r0, bsz), ls]
            bi = xi_ref[pl.ds(r0, bsz), ls]
            nr = ar * xr - ai * xi + br
            ni = ar * xi + ai * xr + bi
            xr_ref[pl.ds(r0, bsz), ls] = nr
            xi_ref[pl.ds(r0, bsz), ls] = ni
            return nr, ni

        xr, xi = lax.fori_loop(0, steps, body, (sr_ref[:, ls], si_ref[:, ls]))
        sr_ref[:, ls] = xr
        si_ref[:, ls] = xi

    xrb = xr_ref[...].astype(BF16)
    xib = xi_ref[...].astype(BF16)
    for j in range(nblk):
        ks = slice(j * kst, (j + 1) * kst)
        y_ref[0, j] = (
            jnp.dot(xrb[:, ks], cre_ref[0, j], preferred_element_type=F32)
            - jnp.dot(xib[:, ks], cim_ref[0, j], preferred_element_type=F32))


def _s5_scan(u2, bre, bim, are, aim, cre, cim, *, bsz, nct_steps):
    nlt, rows, _ = u2.shape
    assert bre.shape[1:3] == (nlt, LANES) and cre.shape[1] == nlt and cre.shape[3] == LANES
    tr = S5_STEPS * bsz
    ns = rows // tr
    nstate = are.shape[-1]
    tile = lambda d, s: _scan_tile(d, s, nct_steps, ns)
    par = lambda a: pl.BlockSpec((1,) + a.shape[1:], lambda d, s: (d,) + (0,) * (a.ndim - 1))
    kern = functools.partial(_s5_scan_kernel, steps=S5_STEPS, bsz=bsz)
    return pl.pallas_call(
        kern,
        out_shape=jax.ShapeDtypeStruct((2, nlt, rows, LANES), F32),
        grid=(2, ns),
        in_specs=[pl.BlockSpec((nlt, tr, LANES), lambda d, s: (0, tile(d, s), 0)),
                  par(bre), par(bim), par(are), par(aim), par(cre), par(cim)],
        out_specs=pl.BlockSpec((1, nlt, tr, LANES), lambda d, s: (d, 0, tile(d, s), 0)),
        scratch_shapes=[pltpu.VMEM((tr, nstate), F32), pltpu.VMEM((tr, nstate), F32),
                        pltpu.VMEM((bsz, nstate), F32), pltpu.VMEM((bsz, nstate), F32)],
        compiler_params=_cparams("parallel", "arbitrary"),
        name="s5_scan",
    )(u2, bre, bim, are, aim, cre, cim)


def _s5_out_kernel(x_ref, mod_ref, u_ref, yf_ref, yb_ref, dsk_ref, wa_ref, wg_ref, o_ref, *, nct, ctx_row):
    d = x_ref.shape[-1]
    n = x_ref.shape[1]
    b = pl.program_id(1)
    row = jnp.where(pl.program_id(0) < nct, ctx_row, b)
    rows = pl.ds(b, n, stride=ctx_row)
    gather = lambda ref: jnp.concatenate([ref[c, rows, :] for c in range(ref.shape[0])], axis=-1)
    y = dsk_ref[...] * gather(u_ref) + gather(yf_ref.at[0]) + gather(yb_ref.at[0])
    z = 0.5 * y * (1.0 + jnp.tanh(math.sqrt(2.0 / math.pi) * (y + 0.044715 * (y * y * y))))
    zb = z.astype(BF16)
    a = jnp.dot(zb, wa_ref[...], preferred_element_type=F32)
    gt = jnp.dot(zb, wg_ref[...], preferred_element_type=F32)
    o_ref[0] = x_ref[0] + _mod_row(mod_ref, row, 5, d) * (a * _sigmoid(gt))


def _s5_out(xs, mod, u_tm, y_tm, d_skip, wa, wg, *, nct):
    bsz, lt, d = xs.shape
    nt = lt // TILE
    sd = d_skip.shape[-1]
    nlt = sd // LANES
    kern = functools.partial(_s5_out_kernel, nct=nct, ctx_row=bsz)
    return pl.pallas_call(
        kern,
        out_shape=jax.ShapeDtypeStruct(xs.shape, F32),
        grid=(nt, bsz),
        in_specs=[pl.BlockSpec((1, TILE, d), lambda t, b: (b, t, 0)), _full(mod.shape),
                  pl.BlockSpec((nlt, TILE * bsz, LANES), lambda t, b: (0, t, 0)),
                  pl.BlockSpec((1, nlt, TILE * bsz, LANES), lambda t, b: (0, 0, t, 0)),
                  pl.BlockSpec((1, nlt, TILE * bsz, LANES), lambda t, b: (1, 0, t, 0)),
                  _full((1, sd)), _full(wa.shape), _full(wg.shape)],
        out_specs=pl.BlockSpec((1, TILE, d), lambda t, b: (b, t, 0)),
        input_output_aliases={0: 0},
        compiler_params=_cparams("parallel", "arbitrary"),
        name="s5_out",
    )(xs, mod, u_tm, y_tm, y_tm, d_skip.reshape(1, sd), wa, wg)


def _s5_params(lam_re, lam_im, log_step, b_re, b_im, c_re, c_im, bsz):
    lam_re = jnp.minimum(lam_re, -1e-4)
    dt = jnp.exp(log_step)[..., None]
    mag = jnp.exp(lam_re * dt)
    ab_re = mag * jnp.cos(lam_im * dt)
    ab_im = mag * jnp.sin(lam_im * dt)
    den = lam_re * lam_re + lam_im * lam_im
    num_re = ab_re - 1.0
    coef_re = (num_re * lam_re + ab_im * lam_im) / den
    coef_im = (ab_im * lam_re - num_re * lam_im) / den
    bb_re = coef_re[..., None] * b_re - coef_im[..., None] * b_im
    bb_im = coef_re[..., None] * b_im + coef_im[..., None] * b_re
    ndir, g, p, h = bb_re.shape
    gb = S5_BLOCK_GROUPS
    nb = g // gb
    eye = jnp.eye(gb, dtype=F32)

    def in_blocks(bb):
        bb = bb.reshape(ndir, nb, gb, p, h)
        return jnp.einsum('djaph,ab->djahbp', bb, eye).reshape(ndir, nb, gb * h, gb * p).astype(BF16)

    def out_blocks(cc):
        cc = cc.reshape(ndir, nb, gb, h, p)
        return jnp.einsum('djahp,ab->djapbh', cc, eye).reshape(ndir, nb, gb * p, gb * h).astype(BF16)

    bcast = lambda a: jnp.broadcast_to(a.reshape(ndir, 1, g * p), (ndir, bsz, g * p))
    return (in_blocks(bb_re), in_blocks(bb_im), bcast(ab_re), bcast(ab_im),
            out_blocks(c_re), out_blocks(c_im))


def kernel(x, c, ctx, c_ctx, w_mod, b_mod, norm_g, ffn_w13, ffn_w2, final_g, e_w_in, e_conv_w, e_conv_b, e_cln_g, e_cln_b, e_sconv_w, e_a_log, e_dt_bias, e_onorm_g, e_w_out, o_w_in, o_lam_re, o_lam_im, o_log_step, o_b_re, o_b_im, o_c_re, o_c_im, o_d, o_w_out):
    bsz, seq, d = x.shape
    lc = ctx.shape[1]
    depth = w_mod.shape[0]
    assert lc == TILE and seq % TILE == 0 and bsz == SUBLANES and bsz < MOD_ROWS
    nct = lc // TILE
    lt = lc + seq
    nt = lt // TILE
    f = ffn_w2.shape[2]
    cdim = e_conv_w.shape[-1]
    gdim = e_w_out.shape[1] - cdim
    nh = gdim // GDN_DK
    n_ab = e_w_in.shape[-1] - 2 * cdim - 4 * gdim

    cc = jnp.zeros((MOD_ROWS, d), F32).at[:bsz].set(c).at[bsz].set(c_ctx)
    mod = _mod_table(cc, w_mod, b_mod)
    xs = jnp.concatenate([ctx, x], axis=1)

    w13 = ffn_w13.astype(BF16)
    w2 = ffn_w2.astype(BF16)

    for l in range(depth):
        last = l == depth - 1
        j = l // 2
        ml = mod[l]
        xs = _ffn(xs, ml, norm_g[l, 0], w13[l, 0], w2[l, 0], final_g, nct=nct, k0=0, final=False)
        if l % 2 == 0:
            w_in = e_w_in[j].astype(BF16)
            c0, c1, c2 = 2 * cdim, 2 * cdim + 3 * gdim, 2 * cdim + 4 * gdim
            wab = jnp.zeros((d, LANES), BF16).at[:, :n_ab].set(w_in[:, c2:])
            conv, qkv, gate, ab = _even_in(xs, ml, norm_g[l, 1], w_in[:, :c0], w_in[:, c0:c1], w_in[:, c1:c2], wab,
                                           e_conv_w[j], e_conv_b[j], e_cln_g[j], e_cln_b[j], nct=nct, n_ab=n_ab)
            alog_row = jnp.zeros((2, 2, nh), F32).at[:, 0].set(e_a_log[j]).reshape(1, n_ab)
            dtb_row = jnp.zeros((2, 2, nh), F32).at[:, 0].set(e_dt_bias[j]).reshape(1, n_ab)
            q, k, v, gf, gb, gt, beta = _gdn_prep(qkv, ab, e_sconv_w[j], alog_row, dtb_row, nct=nct)
            pick = lambda a, dr, jj: a.reshape(bsz, lt, 2, 2, nh)[:, :, dr, jj]
            gcol = jnp.stack([jnp.concatenate([pick(gf, 0, 0), pick(beta, 0, 1), pick(gt, 0, 0)], axis=-1),
                              jnp.concatenate([pick(gb, 1, 0), pick(beta, 1, 1), pick(gt, 1, 0)], axis=-1)])
            grow = jnp.swapaxes(gcol.reshape(2, bsz, nt, TILE, 3 * nh), -1, -2)
            o2 = _gdn(q, k, v, gcol, grow, nct=nct)
            w_out = e_w_out[j].astype(BF16)
            xs = _even_out(xs, ml, conv, o2, gate, jnp.tile(e_onorm_g[j], 1), w_out[:cdim], w_out[cdim:], nct=nct)
        else:
            u_tm = _s5_in(xs, ml, norm_g[l, 1], o_w_in[j].astype(BF16), nct=nct)
            params = _s5_params(o_lam_re[j], o_lam_im[j], o_log_step[j], o_b_re[j], o_b_im[j],
                                o_c_re[j], o_c_im[j], bsz)
            y_tm = _s5_scan(u_tm, *params, bsz=bsz, nct_steps=lc // S5_STEPS)
            w_o = o_w_out[j].astype(BF16)
            xs = _s5_out(xs, ml, u_tm, y_tm, o_d[j], w_o[:, :d], w_o[:, d:], nct=nct)
        xs = _ffn(xs, ml, norm_g[l, 2], w13[l, 1], w2[l, 1], final_g, nct=nct, k0=6, final=last)
    return xs
```

```python
import functools
import math

import jax
import jax.numpy as jnp
from jax import lax
from jax.experimental import pallas as pl
from jax.experimental.pallas import tpu as pltpu

EPS = 1e-6
TILE = 256
LANES = 128
SUBLANES = 8
GRID_W = 64
GDN_HEADS = 4
GDN_DK = 128
S5_GROUP_CH = 16
S5_STATE = 64
S5_BLOCK_GROUPS = 8
S5_STEPS = 64
MOD_ROWS = 16
VMEM_LIMIT = 56 * 1024 * 1024

F32 = jnp.float32
BF16 = jnp.bfloat16


def _cparams(*sem):
    return pltpu.CompilerParams(dimension_semantics=sem, vmem_limit_bytes=VMEM_LIMIT)


def _dot(a, b):
    return jnp.dot(a.astype(BF16), b.astype(BF16), preferred_element_type=F32)


def _dot_nt(a, b):
    return lax.dot_general(a.astype(BF16), b.astype(BF16), (((1,), (1,)), ((), ())),
                           preferred_element_type=F32)


def _dot_exact(a, b):
    return jnp.dot(a, b, precision=lax.Precision.HIGHEST, preferred_element_type=F32)


def _sigmoid(x):
    return 1.0 / (1.0 + jnp.exp(-x))


def _silu(x):
    return x * _sigmoid(x)


def _mod_row(mod_ref, row, k, d):
    return mod_ref[pl.ds(row, 1), pl.ds(k * d, d)]


def _modulated_norm(x, g, shift, scale):
    y = x * lax.rsqrt(jnp.mean(x * x, axis=-1, keepdims=True) + EPS)
    return (y * g) * (1.0 + scale) + shift


def _full(shape):
    n = len(shape)
    return pl.BlockSpec(shape, lambda *_: (0,) * n, pipeline_mode=pl.Buffered(1))


def _mod_kernel(c_ref, w_ref, b_ref, o_ref):
    h = _silu(c_ref[...])
    o_ref[0] = _dot(h, w_ref[0]) + b_ref[0]


def _mod_table(cc, w_mod, b_mod):
    depth, d, nd = w_mod.shape
    tn = 1024
    return pl.pallas_call(
        _mod_kernel,
        out_shape=jax.ShapeDtypeStruct((depth, MOD_ROWS, nd), F32),
        grid=(depth, nd // tn),
        in_specs=[pl.BlockSpec((MOD_ROWS, d), lambda l, j: (0, 0)),
                  pl.BlockSpec((1, d, tn), lambda l, j: (l, 0, j)),
                  pl.BlockSpec((1, 1, tn), lambda l, j: (l, 0, j))],
        out_specs=pl.BlockSpec((1, MOD_ROWS, tn), lambda l, j: (l, 0, j)),
        compiler_params=_cparams("parallel", "parallel"),
        name="mod_table",
    )(cc, w_mod, b_mod.reshape(depth, 1, nd))


FFN_CHUNK = 256


def _ffn_tile(x, row, mod_ref, g_ref, w13_ref, w2_ref, t_ref, k0):
    d = x.shape[-1]
    f = w2_ref.shape[0]
    h = _modulated_norm(x, g_ref[...], _mod_row(mod_ref, row, k0, d), _mod_row(mod_ref, row, k0 + 1, d))
    hb = h.astype(BF16)
    for c in range(f // FFN_CHUNK):
        sl = slice(c * FFN_CHUNK, (c + 1) * FFN_CHUNK)
        a = jnp.dot(hb, w13_ref[:, sl], preferred_element_type=F32)
        b = jnp.dot(hb, w13_ref[:, f + c * FFN_CHUNK:f + (c + 1) * FFN_CHUNK], preferred_element_type=F32)
        t_ref[:, sl] = (_silu(a) * b).astype(BF16)
    y = jnp.dot(t_ref[...], w2_ref[...], preferred_element_type=F32)
    return x + (0.5 * _mod_row(mod_ref, row, k0 + 2, d)) * y


def _tile_ids(nct, ctx_row, t_off):
    t = pl.program_id(0) + t_off
    b = pl.program_id(1)
    is_ctx = t < nct
    return t, b, is_ctx, jnp.where(is_ctx, ctx_row, b)


def _depthwise_rows(h, pad_ref, w_ref, seg, joined):
    n, _ = h.shape
    kw = w_ref.shape[0]
    half = kw // 2
    lead = -(-half // SUBLANES) * SUBLANES
    stride = seg + 2 * lead
    nseg = n // seg
    rows = nseg * stride
    zeros = jnp.zeros((lead, h.shape[1]), F32)
    for s in range(nseg):
        base = s * stride
        before = jnp.where(joined, h[s * seg - lead:s * seg], zeros) if s > 0 else zeros
        after = jnp.where(joined, h[(s + 1) * seg:(s + 1) * seg + lead], zeros) if s < nseg - 1 else zeros
        pad_ref[0, base:base + lead, :] = before
        pad_ref[0, base + lead:base + lead + seg, :] = h[s * seg:(s + 1) * seg]
        pad_ref[0, base + lead + seg:base + stride, :] = after
    for r in range(1, SUBLANES):
        pad_ref[r, 0:rows - SUBLANES, :] = pad_ref[0, r:rows - SUBLANES + r, :]
    outs = []
    for s in range(nseg):
        acc = None
        for k in range(kw):
            a, r = divmod(s * stride + lead - half + k, SUBLANES)
            term = w_ref[k:k + 1, :] * pad_ref[r, a * SUBLANES:a * SUBLANES + seg, :]
            acc = term if acc is None else acc + term
        outs.append(acc)
    return outs[0] if len(outs) == 1 else jnp.concatenate(outs, axis=0)


def _even_head_kernel(x_ref, mod_ref, g0_ref, w13_ref, w2_ref, g1_ref,
                      wglu_ref, wqkv_ref, wgate_ref, wab_ref, cw_ref, cb_ref, lg_ref, lb_ref,
                      x1_ref, conv_ref, qkv_ref, gate_ref, ab_ref, t_ref, pad_ref, *, nct, ctx_row):
    d = x_ref.shape[-1]
    cdim = cw_ref.shape[-1]
    _, _, is_ctx, row = _tile_ids(nct, ctx_row, 0)
    x1 = _ffn_tile(x_ref[0], row, mod_ref, g0_ref, w13_ref, w2_ref, t_ref, 0)
    x1_ref[0] = x1
    h = _modulated_norm(x1, g1_ref[...], _mod_row(mod_ref, row, 3, d), _mod_row(mod_ref, row, 4, d))
    hb = h.astype(BF16)
    qkv_ref[0] = jnp.dot(hb, wqkv_ref[...], preferred_element_type=F32)
    gate_ref[0] = jnp.dot(hb, wgate_ref[...], preferred_element_type=F32)
    ab_ref[0] = jnp.dot(hb, wab_ref[...], preferred_element_type=F32)[:, :ab_ref.shape[-1]]
    glu = jnp.dot(hb, wglu_ref[...], preferred_element_type=F32)
    hc = glu[:, :cdim] * _sigmoid(glu[:, cdim:])
    y = _depthwise_rows(hc, pad_ref, cw_ref, GRID_W, is_ctx) + cb_ref[...]
    yc = y - jnp.mean(y, axis=-1, keepdims=True)
    yn = yc * lax.rsqrt(jnp.mean(yc * yc, axis=-1, keepdims=True) + EPS)
    conv_ref[0] = _silu(yn * lg_ref[...] + lb_ref[...])


def _even_head(xs, mod, g0, w13, w2, g1, wglu, wqkv, wgate, wab, conv_w, conv_b, cln_g, cln_b, *, nct, n_ab):
    bsz, lt, d = xs.shape
    nt = lt // TILE
    f = w2.shape[0]
    cdim = conv_w.shape[-1]
    kw = conv_w.shape[0]
    lead = -(-(kw // 2) // SUBLANES) * SUBLANES
    pad_rows = (TILE // GRID_W) * (GRID_W + 2 * lead)
    tok = lambda n: pl.BlockSpec((1, TILE, n), lambda t, b: (b, t, 0))
    row = lambda a: a.reshape(1, -1)
    kern = functools.partial(_even_head_kernel, nct=nct, ctx_row=bsz)
    return pl.pallas_call(
        kern,
        out_shape=[jax.ShapeDtypeStruct(xs.shape, F32),
                   jax.ShapeDtypeStruct((bsz, lt, cdim), F32),
                   jax.ShapeDtypeStruct((bsz, lt, wqkv.shape[-1]), F32),
                   jax.ShapeDtypeStruct((bsz, lt, wgate.shape[-1]), F32),
                   jax.ShapeDtypeStruct((bsz, lt, n_ab), F32)],
        grid=(nt, bsz),
        in_specs=[tok(d), _full(mod.shape), _full((1, d)), _full(w13.shape), _full(w2.shape), _full((1, d)),
                  _full(wglu.shape), _full(wqkv.shape), _full(wgate.shape), _full(wab.shape),
                  _full(conv_w.shape), _full((1, cdim)), _full((1, cdim)), _full((1, cdim))],
        out_specs=[tok(d), tok(cdim), tok(wqkv.shape[-1]), tok(wgate.shape[-1]), tok(n_ab)],
        scratch_shapes=[pltpu.VMEM((TILE, f), BF16), pltpu.VMEM((SUBLANES, pad_rows, cdim), F32)],
        input_output_aliases={0: 0},
        compiler_params=_cparams("parallel", "arbitrary"),
        name="even_head",
    )(xs, mod, row(g0), w13, w2, row(g1), wglu, wqkv, wgate, wab, conv_w, row(conv_b), row(cln_g), row(cln_b))


def _gdn_prep_kernel(qkv_ref, prev_ref, next_ref, ab_ref, sw_ref, alog_ref, dtb_ref,
                     q_ref, k_ref, v_ref, gf_ref, gb_ref, gt_ref, beta_ref, pad_ref, *, nct, nt):
    t = pl.program_id(1)
    n = qkv_ref.shape[1]
    c = qkv_ref.shape[2]
    kw = sw_ref.shape[0]
    half = kw // 2
    has_prev = jnp.logical_and(t != 0, t != nct)
    has_next = jnp.logical_and(t != nct - 1, t != nt - 1)
    pad_ref[0:SUBLANES, :] = jnp.where(has_prev, prev_ref[0], 0.0)
    pad_ref[SUBLANES:SUBLANES + n, :] = qkv_ref[0]
    pad_ref[SUBLANES + n:2 * SUBLANES + n, :] = jnp.where(has_next, next_ref[0], 0.0)
    base = SUBLANES - half
    acc = sw_ref[0:1, :] * pad_ref[base:base + n, :]
    for k in range(1, kw):
        acc = acc + sw_ref[k:k + 1, :] * pad_ref[base + k:base + k + n, :]
    y = _silu(acc)
    dim = c // 3
    for i, ref in enumerate((q_ref, k_ref, v_ref)):
        part = y[:, i * dim:(i + 1) * dim]
        if i == 2:
            ref[0] = part
            continue
        heads = []
        for hd in range(dim // GDN_DK):
            p = part[:, hd * GDN_DK:(hd + 1) * GDN_DK]
            p = p * lax.rsqrt(jnp.sum(p * p, axis=-1, keepdims=True) + EPS)
            heads.append(p * (GDN_DK ** -0.5) if i == 0 else p)
        ref[0] = jnp.concatenate(heads, axis=-1)
    ab = ab_ref[0]
    z = ab + dtb_ref[...]
    softplus = jnp.maximum(z, 0.0) + jnp.log1p(jnp.exp(-jnp.abs(z)))
    la = -jnp.exp(alog_ref[...]) * softplus
    ii = lax.broadcasted_iota(jnp.int32, (n, n), 0)
    jj = lax.broadcasted_iota(jnp.int32, (n, n), 1)
    gf_ref[0] = _dot_exact(jnp.where(ii >= jj, 1.0, 0.0), la)
    gb_ref[0] = _dot_exact(jnp.where(ii <= jj, 1.0, 0.0), la)
    gt_ref[0] = _dot_exact(jnp.ones((n, n), F32), la)
    beta_ref[0] = _sigmoid(ab)


def _gdn_prep(qkv, ab, sconv_w, alog_row, dtb_row, *, nct):
    bsz, lt, c = qkv.shape
    nt = lt // TILE
    dim = c // 3
    n_ab = ab.shape[-1]
    per = TILE // SUBLANES
    nblk = lt // SUBLANES
    tok = lambda n: pl.BlockSpec((1, TILE, n), lambda b, t: (b, t, 0))
    kern = functools.partial(_gdn_prep_kernel, nct=nct, nt=nt)
    return pl.pallas_call(
        kern,
        out_shape=[jax.ShapeDtypeStruct((bsz, lt, dim), F32)] * 3
        + [jax.ShapeDtypeStruct((bsz, lt, n_ab), F32)] * 4,
        grid=(bsz, nt),
        in_specs=[tok(c),
                  pl.BlockSpec((1, SUBLANES, c), lambda b, t: (b, jnp.maximum(t * per - 1, 0), 0)),
                  pl.BlockSpec((1, SUBLANES, c), lambda b, t: (b, jnp.minimum((t + 1) * per, nblk - 1), 0)),
                  tok(n_ab), _full(sconv_w.shape), _full((1, n_ab)), _full((1, n_ab))],
        out_specs=[tok(dim)] * 3 + [tok(n_ab)] * 4,
        scratch_shapes=[pltpu.VMEM((TILE + 2 * SUBLANES, c), F32)],
        compiler_params=_cparams("parallel", "arbitrary"),
        name="gdn_prep",
    )(qkv, qkv, qkv, ab, sconv_w, alog_row, dtb_row)


def _gdn_kernel(qf_ref, kf_ref, vf_ref, gcf_ref, grf_ref, qb_ref, kb_ref, vb_ref, gcb_ref, grb_ref,
                of_ref, ob_ref, s_ref):
    n = qf_ref.shape[1]
    nh = s_ref.shape[1]
    dk = s_ref.shape[2]

    @pl.when(pl.program_id(1) == 0)
    def _():
        s_ref[...] = jnp.zeros_like(s_ref)

    ii = lax.broadcasted_iota(jnp.int32, (n, n), 0)
    jj = lax.broadcasted_iota(jnp.int32, (n, n), 1)
    diff = [ii - jj, jj - ii]

    def same_block(shift):
        return lax.shift_right_logical(ii, shift) == lax.shift_right_logical(jj, shift)

    ins = [(qf_ref, kf_ref, vf_ref, gcf_ref, grf_ref, of_ref), (qb_ref, kb_ref, vb_ref, gcb_ref, grb_ref, ob_ref)]
    chains = [(dr, hd) for dr in range(2) for hd in range(nh)]
    sl = lambda hd: slice(hd * dk, (hd + 1) * dk)
    qs = [ins[dr][0][0, :, sl(hd)] for dr, hd in chains]
    ks = [ins[dr][1][0, :, sl(hd)] for dr, hd in chains]
    vs = [ins[dr][2][0, :, sl(hd)] for dr, hd in chains]
    gcs = [ins[dr][3][0, 0] for dr in range(2)]
    grs = [ins[dr][4][0, 0, 0] for dr in range(2)]
    g_col = [gcs[dr][:, hd:hd + 1] for dr, hd in chains]
    beta = [gcs[dr][:, nh + hd:nh + hd + 1] for dr, hd in chains]
    gtot_col = [gcs[dr][:, 2 * nh + hd:2 * nh + hd + 1] for dr, hd in chains]
    g_row = [grs[dr][hd:hd + 1, :] for dr, hd in chains]
    gtot_row = [grs[dr][2 * nh + hd:2 * nh + hd + 1, :dk] for dr, hd in chains]
    lower = [diff[dr] >= 0 for dr, _ in chains]
    strict = [diff[dr] > 0 for dr, _ in chains]
    cs = range(len(chains))
    decay = [jnp.where(lower[c], jnp.exp(jnp.where(lower[c], g_col[c] - g_row[c], 0.0)), 0.0) for c in cs]
    kb = [ks[c] * beta[c] for c in cs]
    a_mat = [jnp.where(strict[c], _dot_nt(kb[c], ks[c]) * decay[c], 0.0) for c in cs]
    blk = same_block(3)
    dm = [jnp.where(blk, a_mat[c], 0.0) for c in cs]
    d2 = [_dot(dm[c], dm[c]) for c in cs]
    f = [d2[c] - dm[c] - _dot(dm[c], d2[c]) for c in cs]
    d4 = [_dot(d2[c], d2[c]) for c in cs]
    f = [f[c] + d4[c] + _dot(f[c], d4[c]) for c in cs]
    shift = 3
    while (1 << shift) < n:
        nblk = same_block(shift + 1)
        sel = jnp.logical_and(nblk, jnp.logical_not(blk))
        e = [jnp.where(sel, a_mat[c], 0.0) for c in cs]
        x = [e[c] + _dot(f[c], e[c]) for c in cs]
        f = [f[c] - x[c] - _dot(x[c], f[c]) for c in cs]
        blk = nblk
        shift += 1
    eg = [jnp.exp(g_col[c]) for c in cs]
    rhs = [jnp.concatenate([vs[c] * beta[c], kb[c] * eg[c]], axis=-1) for c in cs]
    uw = [rhs[c] + _dot(f[c], rhs[c]) for c in cs]
    s = [s_ref[dr, hd] for dr, hd in chains]
    v_new = [uw[c][:, :dk] - _dot(uw[c][:, dk:], s[c]) for c in cs]
    qk = [jnp.where(lower[c], _dot_nt(qs[c], ks[c]) * decay[c], 0.0) for c in cs]
    for c, (dr, hd) in enumerate(chains):
        ins[dr][5][0, :, sl(hd)] = _dot(qs[c] * eg[c], s[c]) + _dot(qk[c], v_new[c])
    for c, (dr, hd) in enumerate(chains):
        k_tail = ks[c] * jnp.exp(gtot_col[c] - g_col[c])
        s_ref[dr, hd] = s[c] * jnp.exp(gtot_row[c]) + _dot(k_tail.T, v_new[c])


def _scan_tile(d, s, nct, nt):
    rev = jnp.where(s < nct, nct - 1 - s, nt - 1 - (s - nct))
    return jnp.where(d == 0, s, rev)


def _gdn(q, k, v, gcol, grow, *, nct):
    bsz, lt, dim = q.shape
    nt = lt // TILE
    ng = gcol.shape[-1]

    def specs(dr):
        tile = lambda s: _scan_tile(dr, s, nct, nt)
        tok = pl.BlockSpec((1, TILE, dim), lambda b, s: (b, tile(s), 0))
        return [tok, tok, tok,
                pl.BlockSpec((1, 1, TILE, ng), lambda b, s: (dr, b, tile(s), 0)),
                pl.BlockSpec((1, 1, 1, ng, TILE), lambda b, s: (dr, b, tile(s), 0, 0))], tok

    in_f, out_f = specs(0)
    in_b, out_b = specs(1)
    return pl.pallas_call(
        _gdn_kernel,
        out_shape=[jax.ShapeDtypeStruct((bsz, lt, dim), F32)] * 2,
        grid=(bsz, nt),
        in_specs=in_f + in_b,
        out_specs=[out_f, out_b],
        scratch_shapes=[pltpu.VMEM((2, GDN_HEADS, GDN_DK, dim // GDN_HEADS), F32)],
        compiler_params=_cparams("parallel", "arbitrary"),
        name="gdn_scan",
    )(q, k, v, gcol, grow, q, k, v, gcol, grow)


def _final_norm(x, fg_ref):
    return x * lax.rsqrt(jnp.mean(x * x, axis=-1, keepdims=True) + EPS) * fg_ref[...]


def _even_tail_kernel(x_ref, mod_ref, conv_ref, of_ref, ob_ref, gate_ref, og_ref, wa_ref, wb_ref,
                      g2_ref, w13_ref, w2_ref, fg_ref, o_ref, t_ref, *, nct, ctx_row, t_off, final):
    d = x_ref.shape[-1]
    _, _, _, row = _tile_ids(nct, ctx_row, t_off)
    o = of_ref[0] + ob_ref[0]
    dv = og_ref.shape[-1]
    heads = []
    for hd in range(o.shape[-1] // dv):
        p = o[:, hd * dv:(hd + 1) * dv]
        heads.append(p * lax.rsqrt(jnp.mean(p * p, axis=-1, keepdims=True) + EPS) * og_ref[...])
    gdn = jnp.concatenate(heads, axis=-1) * _silu(gate_ref[0])
    y = _dot(conv_ref[0], wa_ref[...]) + _dot(gdn, wb_ref[...])
    x2 = x_ref[0] + _mod_row(mod_ref, row, 5, d) * y
    x3 = _ffn_tile(x2, row, mod_ref, g2_ref, w13_ref, w2_ref, t_ref, 6)
    o_ref[0] = _final_norm(x3, fg_ref) if final else x3


def _tail_specs(xs, nct, final):
    bsz, lt, d = xs.shape
    nt = lt // TILE
    t_off = nct if final else 0
    tok = lambda n: pl.BlockSpec((1, TILE, n), lambda t, b: (b, t + t_off, 0))
    if final:
        out_shape = jax.ShapeDtypeStruct((bsz, lt - nct * TILE, d), F32)
        out_spec = pl.BlockSpec((1, TILE, d), lambda t, b: (b, t, 0))
        aliases = {}
    else:
        out_shape = jax.ShapeDtypeStruct(xs.shape, F32)
        out_spec = tok(d)
        aliases = {0: 0}
    return (nt - t_off, bsz), t_off, tok, out_shape, out_spec, aliases


def _even_tail(xs, mod, conv, o_f, o_b, gate, onorm_g, wa, wb, g2, w13, w2, final_g, *, nct, final):
    bsz, lt, d = xs.shape
    dim = conv.shape[-1]
    f = w2.shape[0]
    grid, t_off, tok, out_shape, out_spec, aliases = _tail_specs(xs, nct, final)
    row = lambda a: a.reshape(1, -1)
    kern = functools.partial(_even_tail_kernel, nct=nct, ctx_row=bsz, t_off=t_off, final=final)
    return pl.pallas_call(
        kern,
        out_shape=out_shape,
        grid=grid,
        in_specs=[tok(d), _full(mod.shape), tok(dim), tok(dim), tok(dim),
                  tok(dim), _full((1, onorm_g.shape[-1])), _full(wa.shape), _full(wb.shape),
                  _full((1, d)), _full(w13.shape), _full(w2.shape), _full((1, d))],
        out_specs=out_spec,
        scratch_shapes=[pltpu.VMEM((TILE, f), BF16)],
        input_output_aliases=aliases,
        compiler_params=_cparams("parallel", "arbitrary"),
        name="even_tail_final" if final else "even_tail",
    )(xs, mod, conv, o_f, o_b, gate, row(onorm_g), wa, wb, row(g2), w13, w2, row(final_g))


def _s5_head_kernel(x_ref, mod_ref, g0_ref, w13_ref, w2_ref, g1_ref, w_ref, x1_ref, u_ref, t_ref,
                    *, nct, ctx_row):
    d = x_ref.shape[-1]
    n = x_ref.shape[1]
    _, b, _, row = _tile_ids(nct, ctx_row, 0)
    x1 = _ffn_tile(x_ref[0], row, mod_ref, g0_ref, w13_ref, w2_ref, t_ref, 0)
    x1_ref[0] = x1
    h = _modulated_norm(x1, g1_ref[...], _mod_row(mod_ref, row, 3, d), _mod_row(mod_ref, row, 4, d))
    u = _dot(h, w_ref[...])
    for c in range(u_ref.shape[0]):
        u_ref[c, pl.ds(b, n, stride=ctx_row), :] = u[:, c * LANES:(c + 1) * LANES]


def _s5_head(xs, mod, g0, w13, w2, g1, w_in, *, nct):
    bsz, lt, d = xs.shape
    nt = lt // TILE
    f = w2.shape[0]
    sd = w_in.shape[-1]
    tok = pl.BlockSpec((1, TILE, d), lambda t, b: (b, t, 0))
    row = lambda a: a.reshape(1, -1)
    kern = functools.partial(_s5_head_kernel, nct=nct, ctx_row=bsz)
    return pl.pallas_call(
        kern,
        out_shape=[jax.ShapeDtypeStruct(xs.shape, F32),
                   jax.ShapeDtypeStruct((sd // LANES, lt * bsz, LANES), F32)],
        grid=(nt, bsz),
        in_specs=[tok, _full(mod.shape), _full((1, d)), _full(w13.shape), _full(w2.shape), _full((1, d)),
                  _full(w_in.shape)],
        out_specs=[tok, pl.BlockSpec((sd // LANES, TILE * bsz, LANES), lambda t, b: (0, t, 0))],
        scratch_shapes=[pltpu.VMEM((TILE, f), BF16)],
        input_output_aliases={0: 0},
        compiler_params=_cparams("parallel", "arbitrary"),
        name="s5_head",
    )(xs, mod, row(g0), w13, w2, row(g1), w_in)


def _s5_scan_kernel(u_ref, bre_ref, bim_ref, are_ref, aim_ref, cre_ref, cim_ref, y_ref,
                    xr_ref, xi_ref, sr_ref, si_ref, *, steps, bsz):
    d = pl.program_id(0)
    nblk = bre_ref.shape[1]
    kst = bre_ref.shape[3]

    @pl.when(pl.program_id(1) == 0)
    def _():
        sr_ref[...] = jnp.zeros_like(sr_ref)
        si_ref[...] = jnp.zeros_like(si_ref)

    for j in range(nblk):
        uj = u_ref[j].astype(BF16)
        xr_ref[:, j * kst:(j + 1) * kst] = jnp.dot(uj, bre_ref[0, j], preferred_element_type=F32)
        xi_ref[:, j * kst:(j + 1) * kst] = jnp.dot(uj, bim_ref[0, j], preferred_element_type=F32)

    nstate = xr_ref.shape[1]
    half = nstate // 2
    for part in range(2):
        ls = slice(part * half, (part + 1) * half)
        ar = are_ref[0, :, ls]
        ai = aim_ref[0, :, ls]

        def body(i, carry):
            xr, xi = carry
            t = jnp.where(d == 0, i, steps - 1 - i)
            r0 = pl.multiple_of(t * bsz, bsz)
            br = xr_ref[pl.ds(r0, bsz), ls]
            bi = xi_ref[pl.ds(r0, bsz), ls]
            nr = ar * xr - ai * xi + br
            ni = ar * xi + ai * xr + bi
            xr_ref[pl.ds(r0, bsz), ls] = nr
            xi_ref[pl.ds(r0, bsz), ls] = ni
            return nr, ni

        xr, xi = lax.fori_loop(0, steps, body, (sr_ref[:, ls], si_ref[:, ls]))
        sr_ref[:, ls] = xr
        si_ref[:, ls] = xi

    xrb = xr_ref[...].astype(BF16)
    xib = xi_ref[...].astype(BF16)
    for j in range(nblk):
        ks = slice(j * kst, (j + 1) * kst)
        y_ref[0, j] = (
            jnp.dot(xrb[:, ks], cre_ref[0, j], preferred_element_type=F32)
            - jnp.dot(xib[:, ks], cim_ref[0, j], preferred_element_type=F32))


def _s5_scan(u2, bre, bim, are, aim, cre, cim, *, bsz, nct_steps):
    nlt, rows, _ = u2.shape
    assert bre.shape[1:3] == (nlt, LANES) and cre.shape[1] == nlt and cre.shape[3] == LANES
    tr = S5_STEPS * bsz
    ns = rows // tr
    nstate = are.shape[-1]
    tile = lambda d, s: _scan_tile(d, s, nct_steps, ns)
    par = lambda a: pl.BlockSpec((1,) + a.shape[1:], lambda d, s: (d,) + (0,) * (a.ndim - 1))
    kern = functools.partial(_s5_scan_kernel, steps=S5_STEPS, bsz=bsz)
    return pl.pallas_call(
        kern,
        out_shape=jax.ShapeDtypeStruct((2, nlt, rows, LANES), F32),
        grid=(2, ns),
        in_specs=[pl.BlockSpec((nlt, tr, LANES), lambda d, s: (0, tile(d, s), 0)),
                  par(bre), par(bim), par(are), par(aim), par(cre), par(cim)],
        out_specs=pl.BlockSpec((1, nlt, tr, LANES), lambda d, s: (d, 0, tile(d, s), 0)),
        scratch_shapes=[pltpu.VMEM((tr, nstate), F32), pltpu.VMEM((tr, nstate), F32),
                        pltpu.VMEM((bsz, nstate), F32), pltpu.VMEM((bsz, nstate), F32)],
        compiler_params=_cparams("parallel", "arbitrary"),
        name="s5_scan",
    )(u2, bre, bim, are, aim, cre, cim)


def _s5_tail_kernel(x_ref, mod_ref, u_ref, yf_ref, yb_ref, dsk_ref, wa_ref, wg_ref,
                    g2_ref, w13_ref, w2_ref, fg_ref, o_ref, t_ref, *, nct, ctx_row, t_off, final):
    d = x_ref.shape[-1]
    n = x_ref.shape[1]
    _, b, _, row = _tile_ids(nct, ctx_row, t_off)
    rows = pl.ds(b, n, stride=ctx_row)
    gather = lambda ref: jnp.concatenate([ref[c, rows, :] for c in range(ref.shape[0])], axis=-1)
    y = dsk_ref[...] * gather(u_ref) + gather(yf_ref.at[0]) + gather(yb_ref.at[0])
    z = 0.5 * y * (1.0 + jnp.tanh(math.sqrt(2.0 / math.pi) * (y + 0.044715 * (y * y * y))))
    zb = z.astype(BF16)
    a = jnp.dot(zb, wa_ref[...], preferred_element_type=F32)
    gt = jnp.dot(zb, wg_ref[...], preferred_element_type=F32)
    x2 = x_ref[0] + _mod_row(mod_ref, row, 5, d) * (a * _sigmoid(gt))
    x3 = _ffn_tile(x2, row, mod_ref, g2_ref, w13_ref, w2_ref, t_ref, 6)
    o_ref[0] = _final_norm(x3, fg_ref) if final else x3


def _s5_tail(xs, mod, u_tm, y_tm, d_skip, wa, wg, g2, w13, w2, final_g, *, nct, final):
    bsz, lt, d = xs.shape
    sd = d_skip.shape[-1]
    nlt = sd // LANES
    f = w2.shape[0]
    grid, t_off, tok, out_shape, out_spec, aliases = _tail_specs(xs, nct, final)
    row = lambda a: a.reshape(1, -1)
    kern = functools.partial(_s5_tail_kernel, nct=nct, ctx_row=bsz, t_off=t_off, final=final)
    return pl.pallas_call(
        kern,
        out_shape=out_shape,
        grid=grid,
        in_specs=[tok(d), _full(mod.shape),
                  pl.BlockSpec((nlt, TILE * bsz, LANES), lambda t, b: (0, t + t_off, 0)),
                  pl.BlockSpec((1, nlt, TILE * bsz, LANES), lambda t, b: (0, 0, t + t_off, 0)),
                  pl.BlockSpec((1, nlt, TILE * bsz, LANES), lambda t, b: (1, 0, t + t_off, 0)),
                  _full((1, sd)), _full(wa.shape), _full(wg.shape),
                  _full((1, d)), _full(w13.shape), _full(w2.shape), _full((1, d))],
        out_specs=out_spec,
        scratch_shapes=[pltpu.VMEM((TILE, f), BF16)],
        input_output_aliases=aliases,
        compiler_params=_cparams("parallel", "arbitrary"),
        name="s5_tail_final" if final else "s5_tail",
    )(xs, mod, u_tm, y_tm, y_tm, row(d_skip), wa, wg, row(g2), w13, w2, row(final_g))


def _s5_params(lam_re, lam_im, log_step, b_re, b_im, c_re, c_im, bsz):
    lam_re = jnp.minimum(lam_re, -1e-4)
    dt = jnp.exp(log_step)[..., None]
    mag = jnp.exp(lam_re * dt)
    ab_re = mag * jnp.cos(lam_im * dt)
    ab_im = mag * jnp.sin(lam_im * dt)
    den = lam_re * lam_re + lam_im * lam_im
    num_re = ab_re - 1.0
    coef_re = (num_re * lam_re + ab_im * lam_im) / den
    coef_im = (ab_im * lam_re - num_re * lam_im) / den
    bb_re = coef_re[..., None] * b_re - coef_im[..., None] * b_im
    bb_im = coef_re[..., None] * b_im + coef_im[..., None] * b_re
    ndir, g, p, h = bb_re.shape
    gb = S5_BLOCK_GROUPS
    nb = g // gb
    eye = jnp.eye(gb, dtype=F32)

    def in_blocks(bb):
        bb = bb.reshape(ndir, nb, gb, p, h)
        return jnp.einsum('djaph,ab->djahbp', bb, eye).reshape(ndir, nb, gb * h, gb * p).astype(BF16)

    def out_blocks(cc):
        cc = cc.reshape(ndir, nb, gb, h, p)
        return jnp.einsum('djahp,ab->djapbh', cc, eye).reshape(ndir, nb, gb * p, gb * h).astype(BF16)

    bcast = lambda a: jnp.broadcast_to(a.reshape(ndir, 1, g * p), (ndir, bsz, g * p))
    return (in_blocks(bb_re), in_blocks(bb_im), bcast(ab_re), bcast(ab_im),
            out_blocks(c_re), out_blocks(c_im))


def kernel(x, c, ctx, c_ctx, w_mod, b_mod, norm_g, ffn_w13, ffn_w2, final_g, e_w_in, e_conv_w, e_conv_b, e_cln_g, e_cln_b, e_sconv_w, e_a_log, e_dt_bias, e_onorm_g, e_w_out, o_w_in, o_lam_re, o_lam_im, o_log_step, o_b_re, o_b_im, o_c_re, o_c_im, o_d, o_w_out):
    bsz, seq, d = x.shape
    lc = ctx.shape[1]
    depth = w_mod.shape[0]
    assert lc == TILE and seq % TILE == 0 and bsz == SUBLANES and bsz < MOD_ROWS
    nct = lc // TILE
    lt = lc + seq
    nt = lt // TILE
    cdim = e_conv_w.shape[-1]
    gdim = e_w_out.shape[1] - cdim
    nh = gdim // GDN_DK
    n_ab = e_w_in.shape[-1] - 2 * cdim - 4 * gdim

    cc = jnp.zeros((MOD_ROWS, d), F32).at[:bsz].set(c).at[bsz].set(c_ctx)
    mod = _mod_table(cc, w_mod, b_mod)
    xs = jnp.concatenate([ctx, x], axis=1)

    w13 = ffn_w13.astype(BF16)
    w2 = ffn_w2.astype(BF16)

    for l in range(depth):
        last = l == depth - 1
        j = l // 2
        ml = mod[l]
        if l % 2 == 0:
            w_in = e_w_in[j].astype(BF16)
            c0, c1, c2 = 2 * cdim, 2 * cdim + 3 * gdim, 2 * cdim + 4 * gdim
            wab = jnp.zeros((d, LANES), BF16).at[:, :n_ab].set(w_in[:, c2:])
            xs, conv, qkv, gate, ab = _even_head(
                xs, ml, norm_g[l, 0], w13[l, 0], w2[l, 0], norm_g[l, 1],
                w_in[:, :c0], w_in[:, c0:c1], w_in[:, c1:c2], wab,
                e_conv_w[j], e_conv_b[j], e_cln_g[j], e_cln_b[j], nct=nct, n_ab=n_ab)
            alog_row = jnp.zeros((2, 2, nh), F32).at[:, 0].set(e_a_log[j]).reshape(1, n_ab)
            dtb_row = jnp.zeros((2, 2, nh), F32).at[:, 0].set(e_dt_bias[j]).reshape(1, n_ab)
            q, k, v, gf, gb, gt, beta = _gdn_prep(qkv, ab, e_sconv_w[j], alog_row, dtb_row, nct=nct)
            pick = lambda a, dr, jj: a.reshape(bsz, lt, 2, 2, nh)[:, :, dr, jj]
            gcol = jnp.stack([jnp.concatenate([pick(gf, 0, 0), pick(beta, 0, 1), pick(gt, 0, 0)], axis=-1),
                              jnp.concatenate([pick(gb, 1, 0), pick(beta, 1, 1), pick(gt, 1, 0)], axis=-1)])
            grow = jnp.swapaxes(gcol.reshape(2, bsz, nt, TILE, 3 * nh), -1, -2)
            o_f, o_b = _gdn(q, k, v, gcol, grow, nct=nct)
            w_out = e_w_out[j].astype(BF16)
            xs = _even_tail(xs, ml, conv, o_f, o_b, gate, e_onorm_g[j], w_out[:cdim], w_out[cdim:],
                            norm_g[l, 2], w13[l, 1], w2[l, 1], final_g, nct=nct, final=last)
        else:
            xs, u_tm = _s5_head(xs, ml, norm_g[l, 0], w13[l, 0], w2[l, 0], norm_g[l, 1],
                                o_w_in[j].astype(BF16), nct=nct)
            params = _s5_params(o_lam_re[j], o_lam_im[j], o_log_step[j], o_b_re[j], o_b_im[j],
                                o_c_re[j], o_c_im[j], bsz)
            y_tm = _s5_scan(u_tm, *params, bsz=bsz, nct_steps=lc // S5_STEPS)
            w_o = o_w_out[j].astype(BF16)
            xs = _s5_tail(xs, ml, u_tm, y_tm, o_d[j], w_o[:, :d], w_o[:, d:],
                          norm_g[l, 2], w13[l, 1], w2[l, 1], final_g, nct=nct, final=last)
    return xs
```
